```python
import math
import jax
import jax.numpy as jnp
from jax import lax
import numpy as np

D_MODEL = 1024
BATCH = 4
SEQ = 8192
DEPTH = 2
DEC_BATCH = 128
DEC_SEQ = 4
PAST_LEN = 16384
PAGE_SIZE = 128

N_EVEN = (DEPTH + 1) // 2
N_ODD = DEPTH // 2
DIFF_HEADS = 4
DIFF_DIM = 64
DIFF_VDIM = 2 * DIFF_DIM
DIFF_ROT = DIFF_DIM // 4
DIFF_SCALE = DIFF_DIM ** -0.5
MLA_HEADS = 4
MLA_NOPE = 64
MLA_ROPE = 32
MLA_VDIM = 128
Q_LORA = 256
KV_LORA = 256
MLA_SCALE = (MLA_NOPE + MLA_ROPE) ** -0.5
DIFF_WIDTH = DIFF_HEADS * DIFF_VDIM
MLA_WIDTH = MLA_HEADS * MLA_VDIM
MIX_WIDTH_EVEN = DIFF_WIDTH + MLA_WIDTH
EVEN_COLS = (DIFF_HEADS * 2 * DIFF_DIM, DIFF_HEADS * 2 * DIFF_DIM, DIFF_WIDTH,
             Q_LORA, KV_LORA, MLA_ROPE, MIX_WIDTH_EVEN)
EVEN_IN = sum(EVEN_COLS)
LRU_WIDTH = D_MODEL
LRU_BLOCKS = 4
LRU_BLOCK = LRU_WIDTH // LRU_BLOCKS
CONV_WIDTH = 4
LRU_C = 8.0
ROPE_THETA = 500000.0
Q_BLOCK = 128
EPS = 1e-6

kernel_name = 'hybrid_diffattn_mla_rglru_step'


def _rms_norm(x, g):
    xf = x.astype(jnp.float32)
    y = xf * lax.rsqrt(jnp.mean(xf * xf, axis=-1, keepdims=True) + EPS)
    return (y * g.astype(jnp.float32)).astype(x.dtype)


def _rope_tables(pos, rot_dim):
    inv_freq = ROPE_THETA ** (-jnp.arange(0, rot_dim, 2, dtype=jnp.float32) / rot_dim)
    ang = pos.astype(jnp.float32)[:, None] * inv_freq[None, :]
    return jnp.cos(ang), jnp.sin(ang)


def _apply_rope(x, cos, sin):
    half = cos.shape[-1]
    shape = (cos.shape[0],) + (1,) * (x.ndim - 3) + (half,)
    c, s = cos.reshape(shape), sin.reshape(shape)
    x1 = x[..., :half].astype(jnp.float32)
    x2 = x[..., half:2 * half].astype(jnp.float32)
    rot = jnp.concatenate([x1 * c - x2 * s, x2 * c + x1 * s], axis=-1).astype(x.dtype)
    return jnp.concatenate([rot, x[..., 2 * half:]], axis=-1)


def _causal_attention(q, k, v, scale):
    B, S, H, D = q.shape
    nb = S // Q_BLOCK
    qb = jnp.moveaxis(q.reshape(B, nb, Q_BLOCK, H, D), 1, 0)
    vf = v.astype(jnp.float32)
    key_pos = jnp.arange(S)

    def block(args):
        i, qi = args
        s = jnp.einsum('bqhd,bkhd->bhqk', qi, k, preferred_element_type=jnp.float32) * scale
        q_pos = i * Q_BLOCK + jnp.arange(Q_BLOCK)
        s = jnp.where(key_pos[None, :] <= q_pos[:, None], s, -jnp.inf)
        p = jax.nn.softmax(s, axis=-1)
        return jnp.einsum('bhqk,bkhv->bqhv', p, vf)

    out = lax.map(block, (jnp.arange(nb), qb))
    return jnp.moveaxis(out, 0, 1).reshape(B, S, H, v.shape[-1])


def _paged_attention(score, weigh, pools, li, new_rows, page_table):
    s_new = score(new_rows)
    T = s_new.shape[-1]
    causal = jnp.tril(jnp.ones((T, T), dtype=bool))
    s_new = jnp.where(causal, s_new, -jnp.inf)
    m0 = jnp.max(s_new, axis=-1)
    p0 = jnp.exp(s_new - m0[..., None])
    carry0 = (m0, jnp.sum(p0, axis=-1), weigh(p0, new_rows))

    def step(carry, phys):
        m, l, acc = carry
        blocks = tuple(pool[li, phys] for pool in pools)
        s = score(blocks)
        m_new = jnp.maximum(m, jnp.max(s, axis=-1))
        corr = jnp.exp(m - m_new)
        p = jnp.exp(s - m_new[..., None])
        return (m_new, l * corr + jnp.sum(p, axis=-1), acc * corr[..., None] + weigh(p, blocks)), None

    (m, l, acc), _ = lax.scan(step, carry0, page_table.T)
    return acc / l[..., None]


def _even_inputs(x, pos, lp):
    B, T, _ = x.shape
    h = _rms_norm(x, lp['norm'])
    splits = np.cumsum(EVEN_COLS)[:-1].tolist()
    dq, dk, dv, cq, ckv, kr, gate = jnp.split(h @ lp['w_in'], splits, axis=-1)
    cos_d, sin_d = _rope_tables(pos, DIFF_ROT)
    dq = _apply_rope(_rms_norm(dq.reshape(B, T, DIFF_HEADS, 2, DIFF_DIM), lp['diff_q_norm']), cos_d, sin_d)
    dk = _apply_rope(_rms_norm(dk.reshape(B, T, DIFF_HEADS, 2, DIFF_DIM), lp['diff_k_norm']), cos_d, sin_d)
    dv = dv.reshape(B, T, DIFF_HEADS, DIFF_VDIM)
    q = _rms_norm(cq, lp['q_a_norm']) @ lp['w_uq']
    q = _rms_norm(q.reshape(B, T, MLA_HEADS, MLA_NOPE + MLA_ROPE), lp['mla_q_norm'])
    cos_m, sin_m = _rope_tables(pos, MLA_ROPE)
    q_nope = q[..., :MLA_NOPE]
    q_rope = _apply_rope(q[..., MLA_NOPE:], cos_m, sin_m)
    ckv = _rms_norm(ckv, lp['kv_norm'])
    kr = _apply_rope(_rms_norm(kr, lp['k_rope_norm']), cos_m, sin_m)
    return dq, dk, dv, q_nope, q_rope, ckv, kr, gate


def _even_output(x, av1, av2, mla_o, gate, lp, lam_init):
    B, T, _ = x.shape
    f32 = jnp.float32
    lam = (jnp.exp(jnp.sum(lp['lq1'].astype(f32) * lp['lk1'].astype(f32)))
           - jnp.exp(jnp.sum(lp['lq2'].astype(f32) * lp['lk2'].astype(f32))) + lam_init)
    diff = _rms_norm(av1 - lam * av2, lp['subln']) * (1.0 - lam_init)
    o = jnp.concatenate([diff.reshape(B, T, DIFF_WIDTH), mla_o.reshape(B, T, MLA_WIDTH)], axis=-1)
    o = (o * jax.nn.silu(gate.astype(f32))).astype(x.dtype)
    return x + o @ lp['w_out']


def _even_layer_prompt(x, pos, lp, lam_init):
    B, S, _ = x.shape
    dq, dk, dv, q_nope, q_rope, ckv, kr, gate = _even_inputs(x, pos, lp)
    av1 = _causal_attention(dq[..., 0, :], dk[..., 0, :], dv, DIFF_SCALE)
    av2 = _causal_attention(dq[..., 1, :], dk[..., 1, :], dv, DIFF_SCALE)
    k_nope = jnp.einsum('bsc,chn->bshn', ckv, lp['w_uk'])
    v_m = jnp.einsum('bsc,chv->bshv', ckv, lp['w_uv'])
    q_m = jnp.concatenate([q_nope, q_rope], axis=-1)
    k_m = jnp.concatenate([k_nope, jnp.broadcast_to(kr[:, :, None, :], (B, S, MLA_HEADS, MLA_ROPE))], axis=-1)
    mla_o = _causal_attention(q_m, k_m, v_m, MLA_SCALE)
    y = _even_output(x, av1, av2, mla_o, gate, lp, lam_init)
    return y, (dk, dv, ckv, kr)


def _diff_score(q_j, j):
    def score(blocks):
        return jnp.einsum('bthd,bshd->bhts', q_j, blocks[0][..., j, :],
                          preferred_element_type=jnp.float32) * DIFF_SCALE
    return score


def _weigh_values(p, blocks):
    return jnp.einsum('bhts,bshv->bhtv', p, blocks[1].astype(jnp.float32))


def _even_layer_sample(x, pos, lp, lam_init, pools_diff, pools_mla, li, page_table):
    dq, dk, dv, q_nope, q_rope, ckv, kr, gate = _even_inputs(x, pos, lp)
    av = []
    for j in range(2):
        out = _paged_attention(_diff_score(dq[..., j, :], j), _weigh_values, pools_diff, li, (dk, dv), page_table)
        av.append(jnp.transpose(out, (0, 2, 1, 3)))
    q_lat = jnp.einsum('bthn,chn->bthc', q_nope, lp['w_uk'])

    def mla_score(blocks):
        s = jnp.einsum('bthc,bsc->bhts', q_lat, blocks[0], preferred_element_type=jnp.float32)
        s = s + jnp.einsum('bthr,bsr->bhts', q_rope, blocks[1], preferred_element_type=jnp.float32)
        return s * MLA_SCALE

    def mla_weigh(p, blocks):
        return jnp.einsum('bhts,bsc->bhtc', p, blocks[0].astype(jnp.float32))

    lat = _paged_attention(mla_score, mla_weigh, pools_mla, li, (ckv, kr), page_table)
    mla_o = jnp.einsum('bhtc,chv->bthv', lat, lp['w_uv'].astype(jnp.float32))
    y = _even_output(x, av[0], av[1], mla_o, gate, lp, lam_init)
    return y, (dk, dv, ckv, kr)


def _lru_combine(left, right):
    a1, b1 = left
    a2, b2 = right
    return a1 * a2, a2 * b1 + b2


def _lru_layer(x, conv_buf, h0, lp):
    B, T, _ = x.shape
    f32 = jnp.float32
    hn = _rms_norm(x, lp['norm'])
    xb, gate = jnp.split(hn @ lp['w_in'], 2, axis=-1)
    ext = jnp.concatenate([conv_buf.astype(xb.dtype), xb], axis=1)
    conv = lp['conv_b'] + sum(ext[:, j:j + T] * lp['conv_w'][j] for j in range(CONV_WIDTH))
    xr = conv.reshape(B, T, LRU_BLOCKS, LRU_BLOCK)
    gx = jax.nn.sigmoid(jnp.einsum('btnk,nkj->btnj', xr, lp['w_gx']) + lp['b_gx']).reshape(B, T, LRU_WIDTH)
    ga = jax.nn.sigmoid(jnp.einsum('btnk,nkj->btnj', xr, lp['w_ga']) + lp['b_ga']).reshape(B, T, LRU_WIDTH)
    log_a = LRU_C * ga.astype(f32) * jax.nn.log_sigmoid(lp['lam'].astype(f32))
    a = jnp.exp(log_a)
    b = jnp.sqrt(-jnp.expm1(2.0 * log_a)) * (gx.astype(f32) * conv.astype(f32))
    b = b.at[:, 0].add(a[:, 0] * h0.astype(f32))
    _, hs = lax.associative_scan(_lru_combine, (a, b), axis=1)
    y = (hs * jax.nn.silu(gate.astype(f32))).astype(x.dtype) @ lp['w_out']
    return x + y, ext[:, T:].astype(conv_buf.dtype), hs[:, -1].astype(h0.dtype)


def setup_inputs(seed: int = 0) -> dict:
    key = jax.random.key(seed)
    ks = iter(jax.random.split(key, 48))
    f32 = jnp.float32
    n_pages = PAST_LEN // PAGE_SIZE
    n_used = DEC_BATCH * n_pages
    n_pool = n_used + n_used // 4

    def nrm(shape, scale=1.0):
        return jax.random.normal(next(ks), shape, f32) * scale

    def gain(shape):
        return 1.0 + 0.02 * jax.random.normal(next(ks), shape, f32)

    x_prompt = nrm((BATCH, SEQ, D_MODEL))
    x_sample = nrm((DEC_BATCH, DEC_SEQ, D_MODEL))
    cache_diff_k = nrm((N_EVEN, n_pool, PAGE_SIZE, DIFF_HEADS, 2, DIFF_DIM))
    cache_diff_v = nrm((N_EVEN, n_pool, PAGE_SIZE, DIFF_HEADS, DIFF_VDIM))
    cache_mla_ckv = nrm((N_EVEN, n_pool, PAGE_SIZE, KV_LORA))
    cache_mla_krope = nrm((N_EVEN, n_pool, PAGE_SIZE, MLA_ROPE))
    state_lru_h = nrm((N_ODD, DEC_BATCH, LRU_WIDTH))
    state_lru_conv = nrm((N_ODD, DEC_BATCH, CONV_WIDTH - 1, LRU_WIDTH))
    page_table = jax.random.permutation(next(ks), n_pool)[:n_used].reshape(DEC_BATCH, n_pages).astype(jnp.int32)

    u = jax.random.uniform(next(ks), (N_ODD, LRU_WIDTH), f32, 0.9, 0.999)
    s = u ** (1.0 / LRU_C)
    lru_lambda = jnp.log(s) - jnp.log1p(-s)

    return {
        'x_prompt': x_prompt,
        'x_sample': x_sample,
        'cache_diff_k': cache_diff_k,
        'cache_diff_v': cache_diff_v,
        'cache_mla_ckv': cache_mla_ckv,
        'cache_mla_krope': cache_mla_krope,
        'state_lru_h': state_lru_h,
        'state_lru_conv': state_lru_conv,
        'page_table': page_table,
        'norm_even': gain((N_EVEN, D_MODEL)),
        'w_in_even': nrm((N_EVEN, D_MODEL, EVEN_IN), D_MODEL ** -0.5),
        'diff_q_norm': gain((N_EVEN, DIFF_DIM)),
        'diff_k_norm': gain((N_EVEN, DIFF_DIM)),
        'diff_lambda_q1': nrm((N_EVEN, DIFF_DIM), 0.1),
        'diff_lambda_k1': nrm((N_EVEN, DIFF_DIM), 0.1),
        'diff_lambda_q2': nrm((N_EVEN, DIFF_DIM), 0.1),
        'diff_lambda_k2': nrm((N_EVEN, DIFF_DIM), 0.1),
        'diff_subln': gain((N_EVEN, DIFF_VDIM)),
        'mla_q_a_norm': gain((N_EVEN, Q_LORA)),
        'mla_w_uq': nrm((N_EVEN, Q_LORA, MLA_HEADS * (MLA_NOPE + MLA_ROPE)), Q_LORA ** -0.5),
        'mla_q_norm': gain((N_EVEN, MLA_NOPE + MLA_ROPE)),
        'mla_kv_norm': gain((N_EVEN, KV_LORA)),
        'mla_k_rope_norm': gain((N_EVEN, MLA_ROPE)),
        'mla_w_uk': nrm((N_EVEN, KV_LORA, MLA_HEADS, MLA_NOPE), KV_LORA ** -0.5),
        'mla_w_uv': nrm((N_EVEN, KV_LORA, MLA_HEADS, MLA_VDIM), KV_LORA ** -0.5),
        'w_out_even': nrm((N_EVEN, MIX_WIDTH_EVEN, D_MODEL), MIX_WIDTH_EVEN ** -0.5),
        'norm_odd': gain((N_ODD, D_MODEL)),
        'w_in_odd': nrm((N_ODD, D_MODEL, 2 * LRU_WIDTH), D_MODEL ** -0.5),
        'lru_conv_w': nrm((N_ODD, CONV_WIDTH, LRU_WIDTH), CONV_WIDTH ** -0.5),
        'lru_conv_b': nrm((N_ODD, LRU_WIDTH), 0.01),
        'lru_w_gx': nrm((N_ODD, LRU_BLOCKS, LRU_BLOCK, LRU_BLOCK), LRU_BLOCK ** -0.5),
        'lru_b_gx': nrm((N_ODD, LRU_BLOCKS, LRU_BLOCK), 0.01),
        'lru_w_ga': nrm((N_ODD, LRU_BLOCKS, LRU_BLOCK, LRU_BLOCK), LRU_BLOCK ** -0.5),
        'lru_b_ga': nrm((N_ODD, LRU_BLOCKS, LRU_BLOCK), 0.01),
        'lru_lambda': lru_lambda,
        'w_out_odd': nrm((N_ODD, LRU_WIDTH, D_MODEL), LRU_WIDTH ** -0.5),
    }


def reference(x_prompt, x_sample, cache_diff_k, cache_diff_v, cache_mla_ckv, cache_mla_krope,
              state_lru_h, state_lru_conv, page_table,
              norm_even, w_in_even, diff_q_norm, diff_k_norm, diff_lambda_q1, diff_lambda_k1,
              diff_lambda_q2, diff_lambda_k2, diff_subln, mla_q_a_norm, mla_w_uq, mla_q_norm,
              mla_kv_norm, mla_k_rope_norm, mla_w_uk, mla_w_uv, w_out_even,
              norm_odd, w_in_odd, lru_conv_w, lru_conv_b, lru_w_gx, lru_b_gx, lru_w_ga, lru_b_ga,
              lru_lambda, w_out_odd):
    seq = x_prompt.shape[1]
    past_len = page_table.shape[1] * cache_diff_k.shape[2]
    pos_p = jnp.arange(seq)
    pos_s = past_len + jnp.arange(x_sample.shape[1])
    yp, ys = x_prompt, x_sample
    rows_p, rows_s, lru_p, lru_s = [], [], [], []
    for layer in range(DEPTH):
        if layer % 2 == 0:
            li = layer // 2
            lp = {'norm': norm_even[li], 'w_in': w_in_even[li],
                  'diff_q_norm': diff_q_norm[li], 'diff_k_norm': diff_k_norm[li],
                  'lq1': diff_lambda_q1[li], 'lk1': diff_lambda_k1[li],
                  'lq2': diff_lambda_q2[li], 'lk2': diff_lambda_k2[li], 'subln': diff_subln[li],
                  'q_a_norm': mla_q_a_norm[li], 'w_uq': mla_w_uq[li], 'mla_q_norm': mla_q_norm[li],
                  'kv_norm': mla_kv_norm[li], 'k_rope_norm': mla_k_rope_norm[li],
                  'w_uk': mla_w_uk[li], 'w_uv': mla_w_uv[li], 'w_out': w_out_even[li]}
            lam_init = 0.8 - 0.6 * math.exp(-0.3 * layer)
            yp, r_p = _even_layer_prompt(yp, pos_p, lp, lam_init)
            ys, r_s = _even_layer_sample(ys, pos_s, lp, lam_init, (cache_diff_k, cache_diff_v),
                                         (cache_mla_ckv, cache_mla_krope), li, page_table)
            rows_p.append(r_p)
            rows_s.append(r_s)
        else:
            lo = layer // 2
            lp = {'norm': norm_odd[lo], 'w_in': w_in_odd[lo], 'conv_w': lru_conv_w[lo],
                  'conv_b': lru_conv_b[lo], 'w_gx': lru_w_gx[lo], 'b_gx': lru_b_gx[lo],
                  'w_ga': lru_w_ga[lo], 'b_ga': lru_b_ga[lo], 'lam': lru_lambda[lo],
                  'w_out': w_out_odd[lo]}
            bp = yp.shape[0]
            conv0 = jnp.zeros((bp, CONV_WIDTH - 1, LRU_WIDTH), state_lru_conv.dtype)
            h00 = jnp.zeros((bp, LRU_WIDTH), state_lru_h.dtype)
            yp, conv_p, h_p = _lru_layer(yp, conv0, h00, lp)
            ys, conv_s, h_s = _lru_layer(ys, state_lru_conv[lo], state_lru_h[lo], lp)
            lru_p.append((h_p, conv_p))
            lru_s.append((h_s, conv_s))
    p_dk = jnp.stack([r[0] for r in rows_p])
    p_dv = jnp.stack([r[1] for r in rows_p])
    p_ckv = jnp.stack([r[2] for r in rows_p])
    p_kr = jnp.stack([r[3] for r in rows_p])
    p_h = jnp.stack([r[0] for r in lru_p])
    p_conv = jnp.stack([r[1] for r in lru_p])
    s_dk = jnp.stack([r[0] for r in rows_s])
    s_dv = jnp.stack([r[1] for r in rows_s])
    s_ckv = jnp.stack([r[2] for r in rows_s])
    s_kr = jnp.stack([r[3] for r in rows_s])
    s_h = jnp.stack([r[0] for r in lru_s])
    s_conv = jnp.stack([r[1] for r in lru_s])
    return (yp, ys, p_dk, p_dv, p_ckv, p_kr, p_h, p_conv, s_dk, s_dv, s_ckv, s_kr, s_h, s_conv)
```

```python
import functools
import math

import numpy as np
import jax
import jax.numpy as jnp
from jax import lax
from jax.experimental import pallas as pl
from jax.experimental.pallas import tpu as pltpu

DIFF_HEADS = 4
DIFF_DIM = 64
DIFF_VDIM = 2 * DIFF_DIM
DIFF_ROT = DIFF_DIM // 4
DIFF_SCALE = DIFF_DIM ** -0.5
MLA_HEADS = 4
MLA_NOPE = 64
MLA_ROPE = 32
MLA_VDIM = 128
MLA_SCALE = (MLA_NOPE + MLA_ROPE) ** -0.5
CONV_WIDTH = 4
LRU_C = 8.0
ROPE_THETA = 500000.0
EPS = 1e-6

LANES = 128
SUBLANES = 8
MXU_DIM = 256
VMEM_LIMIT = 52 * 1024 * 1024

TM_IN = 256
TQ_ATTN = 512
TT_LRU = 256
PAGES_PER_STEP = 8

F32 = jnp.float32
BF16 = jnp.bfloat16


def _round_up(n, m):
    return (n + m - 1) // m * m


def _dot(a, b):
    return jnp.dot(a, b, preferred_element_type=F32)


def _dot_nt(a, b):
    return lax.dot_general(a, b, (((1,), (1,)), ((), ())), preferred_element_type=F32)


def _rms(x, inv_n=None):
    ms = jnp.mean(x * x, axis=-1, keepdims=True) if inv_n is None else jnp.sum(x * x, axis=-1, keepdims=True) * inv_n
    return x * lax.rsqrt(ms + EPS)


def _even_in_kernel(x_ref, norm_ref, w_ref, rope_ref, gains_ref, seg64_ref, seg128_ref, wuq_ref, wx_ref,
                    *out_refs, prompt, tm, pos_base, pos_period, pos_div):
    x = x_ref[...]
    h = (_rms(x) * norm_ref[...]).astype(BF16)
    u = _dot(h, w_ref[...])

    row = pl.program_id(0) * tm + lax.broadcasted_iota(jnp.int32, (tm, LANES), 0)
    pos = (row // pos_period if pos_div else row % pos_period) + pos_base
    posf = pos.astype(F32)

    def tables(k):
        ang = posf * rope_ref[3 * k:3 * k + 1, :]
        s = jnp.sin(ang)
        return jnp.cos(ang), s * rope_ref[3 * k + 1:3 * k + 2, :], s * rope_ref[3 * k + 2:3 * k + 3, :]

    def rope(v, tabs, half):
        c, sa, sb = tabs
        return v * c + pltpu.roll(v, half, 1) * sa + pltpu.roll(v, LANES - half, 1) * sb

    def seg_sum(v, seg_ref):
        sq = v * v
        hi = sq.astype(BF16)
        lo = (sq - hi.astype(F32)).astype(BF16)
        return _dot(hi, seg_ref[...]) + _dot(lo, seg_ref[...])

    def head_norm_rope(v, seg_ref, inv_n, gain, tabs, half):
        slabs = []
        for c in range(v.shape[1] // MXU_DIM):
            vc = v[:, MXU_DIM * c:MXU_DIM * (c + 1)]
            y = vc * lax.rsqrt(seg_sum(vc, seg_ref) * inv_n + EPS)
            for s in range(MXU_DIM // LANES):
                slabs.append(rope(y[:, LANES * s:LANES * (s + 1)] * gain, tabs, half))
        return slabs

    tab_d = tables(0)
    tab_m = tables(1)
    dq = head_norm_rope(u[:, 0:512], seg64_ref, 1.0 / DIFF_DIM, gains_ref[0:1, 0:LANES], tab_d, DIFF_ROT // 2)
    dk = head_norm_rope(u[:, 512:1024], seg64_ref, 1.0 / DIFF_DIM, gains_ref[1:2, 0:LANES], tab_d, DIFF_ROT // 2)
    dv = u[:, 1024:1536]
    cqn = _rms(u[:, 1536:1792]) * gains_ref[4:5, :]
    q = _dot(cqn.astype(BF16), wuq_ref[...])
    qm = head_norm_rope(q, seg128_ref, 1.0 / (MLA_NOPE + MLA_ROPE), gains_ref[2:3, 0:LANES], tab_m, MLA_ROPE // 2)
    ckv = _rms(u[:, 1792:2048]) * gains_ref[5:6, :]
    gate = u[:, 2048:3072]
    kr_slab = rope(_rms(u[:, 3072:3200], 1.0 / MLA_ROPE) * gains_ref[3:4, 0:LANES], tab_m, MLA_ROPE // 2)
    kr = pltpu.roll(kr_slab, LANES - MLA_NOPE, 1)[:, 0:MLA_ROPE]

    def put(ref, slabs):
        for k, s in enumerate(slabs):
            ref[:, LANES * k:LANES * (k + 1)] = s.astype(ref.dtype)

    if prompt:
        dq_o, dk_o, dv_o, dkb_o, dvb_o, qm_o, ckv_o, kr_o, km_o, vm_o, gate_o = out_refs
        put(dq_o, dq)
        put(dk_o, dk)
        put(dkb_o, dk)
        dv_o[...] = dv
        dvb_o[...] = dv.astype(BF16)
        put(qm_o, qm)
        ckv_o[...] = ckv
        kr_o[...] = kr
        kv = _dot(ckv.astype(BF16), wx_ref[...])
        put(km_o, [kv[:, LANES * hh:LANES * (hh + 1)] + kr_slab for hh in range(MLA_HEADS)])
        vm_o[...] = kv[:, 512:1024].astype(BF16)
        gate_o[...] = gate
    else:
        dq_o, dk_o, dv_o, qlat_o, qrope_o, ckv_o, kr_o, gate_o = out_refs
        put(dq_o, dq)
        put(dk_o, dk)
        dv_o[...] = dv
        ckv_o[...] = ckv
        kr_o[...] = kr
        gate_o[...] = gate
        lane = lax.broadcasted_iota(jnp.int32, (tm, LANES), 1)
        qr = jnp.zeros((tm, LANES), F32)
        for hh in range(MLA_HEADS):
            qlat_o[:, MXU_DIM * hh:MXU_DIM * (hh + 1)] = _dot(qm[hh].astype(BF16), wx_ref[hh])
            moved = pltpu.roll(qm[hh], (MLA_ROPE * hh + LANES - MLA_NOPE) % LANES, 1)
            qr = qr + jnp.where((lane >= MLA_ROPE * hh) & (lane < MLA_ROPE * (hh + 1)), moved, 0.0)
        qrope_o[...] = qr


def _even_in_call(x2d, params, *, prompt, pos_base, pos_period, pos_div):
    n, d = x2d.shape
    tm = min(TM_IN, n)
    assert n % tm == 0
    (norm, w, rope_c, gains, seg64, seg128, wuq, wx) = params
    full = lambda a: pl.BlockSpec(a.shape, lambda i, nd=a.ndim: (0,) * nd)
    tile = lambda c: pl.BlockSpec((tm, c), lambda i: (i, 0))
    if prompt:
        outs = [(512, BF16), (512, F32), (512, F32), (512, BF16), (512, BF16), (512, BF16), (256, F32),
                (MLA_ROPE, F32), (512, BF16), (512, BF16), (1024, F32)]
    else:
        outs = [(512, F32), (512, F32), (512, F32), (1024, F32), (LANES, F32), (256, F32), (MLA_ROPE, F32),
                (1024, F32)]
    kern = functools.partial(_even_in_kernel, prompt=prompt, tm=tm, pos_base=pos_base, pos_period=pos_period,
                             pos_div=pos_div)
    return pl.pallas_call(
        kern,
        grid=(n // tm,),
        in_specs=[tile(d), full(norm), full(w), full(rope_c), full(gains), full(seg64), full(seg128), full(wuq),
                  full(wx)],
        out_specs=[tile(c) for c, _ in outs],
        out_shape=[jax.ShapeDtypeStruct((n, c), t) for c, t in outs],
        compiler_params=pltpu.CompilerParams(dimension_semantics=("arbitrary",), vmem_limit_bytes=VMEM_LIMIT),
        name="even_in_prompt" if prompt else "even_in_sample",
    )(x2d, norm, w, rope_c, gains, seg64, seg128, wuq, wx)


def _silu(x):
    return x * jax.nn.sigmoid(x)


def _flash_kernel(q_ref, k_ref, v_ref, gate_ref, lamp_ref, subln_ref, o_ref, m_sc, l_sc, acc_sc, *,
                  n_maps, tq, lam_init):
    i = pl.program_id(2)
    q = q_ref[...]
    if n_maps == 2:
        lane = lax.broadcasted_iota(jnp.int32, q.shape, 1)
        qf = q.astype(F32)
        qs = [jnp.where(lane < DIFF_DIM, qf, 0.0).astype(BF16), jnp.where(lane >= DIFF_DIM, qf, 0.0).astype(BF16)]
    else:
        qs = [q]
    m_sc[...] = jnp.full(m_sc.shape, -jnp.inf, F32)
    l_sc[...] = jnp.zeros(l_sc.shape, F32)
    acc_sc[...] = jnp.zeros(acc_sc.shape, F32)

    def chunk(j, masked):
        rows = pl.ds(pl.multiple_of(j * tq, tq), tq)
        k = k_ref[rows, :]
        v = v_ref[rows, :]
        for a in range(n_maps):
            s = _dot_nt(qs[a], k)
            if masked:
                r_id = lax.broadcasted_iota(jnp.int32, s.shape, 0)
                c_id = lax.broadcasted_iota(jnp.int32, s.shape, 1)
                s = jnp.where(c_id <= r_id, s, -jnp.inf)
            m_prev = m_sc[a]
            m_new = jnp.maximum(m_prev, jnp.max(s, axis=1, keepdims=True))
            alpha = jnp.exp(m_prev - m_new)
            p = jnp.exp(s - m_new)
            l_sc[a] = alpha * l_sc[a] + jnp.sum(p, axis=1, keepdims=True)
            acc_sc[a] = alpha * acc_sc[a] + _dot(p.astype(BF16), v)
            m_sc[a] = m_new

    def body(j, c):
        chunk(j, False)
        return c

    lax.fori_loop(0, i, body, 0)
    chunk(i, True)

    g = _silu(gate_ref[...])
    if n_maps == 2:
        lp = lamp_ref[...]
        lam = (jnp.exp(jnp.sum(lp[0:1] * lp[1:2], axis=1, keepdims=True))
               - jnp.exp(jnp.sum(lp[2:3] * lp[3:4], axis=1, keepdims=True)) + lam_init)
        d = acc_sc[0] / l_sc[0] - lam * (acc_sc[1] / l_sc[1])
        o = (_rms(d) * subln_ref[...]) * (1.0 - lam_init)
    else:
        o = acc_sc[0] / l_sc[0]
    o_ref[...] = (o * g).astype(o_ref.dtype)


def _flash_call(q, k, v, gate, lamp, subln, *, batch, seq, n_maps, gate_col0, lam_init, name):
    n, width = q.shape
    heads = width // LANES
    tq = min(TQ_ATTN, seq)
    assert seq % tq == 0
    nq = seq // tq
    kern = functools.partial(_flash_kernel, n_maps=n_maps, tq=tq, lam_init=lam_init)
    qspec = pl.BlockSpec((tq, LANES), lambda b, h, i: (b * nq + i, h))
    kvspec = pl.BlockSpec((seq, LANES), lambda b, h, i: (b, h))
    return pl.pallas_call(
        kern,
        grid=(batch, heads, nq),
        in_specs=[qspec, kvspec, kvspec,
                  pl.BlockSpec((tq, LANES), lambda b, h, i: (b * nq + i, gate_col0 + h)),
                  pl.BlockSpec(lamp.shape, lambda b, h, i: (0, 0)),
                  pl.BlockSpec(subln.shape, lambda b, h, i: (0, 0))],
        out_specs=qspec,
        out_shape=jax.ShapeDtypeStruct((n, width), BF16),
        scratch_shapes=[pltpu.VMEM((n_maps, tq, 1), F32), pltpu.VMEM((n_maps, tq, 1), F32),
                        pltpu.VMEM((n_maps, tq, LANES), F32)],
        compiler_params=pltpu.CompilerParams(dimension_semantics=("arbitrary",) * 3, vmem_limit_bytes=VMEM_LIMIT),
        name=name,
    )(q, k, v, gate, lamp, subln)


def _paged_kernel(pt_ref, qd_ref, ql_ref, qr_ref, kn_ref, vn_ref, cn_ref, rn_ref, gate_ref, lamp_ref, subln_ref,
                  wuv_ref, *rest, n_new, tp, pages, lam_init):
    del pt_ref
    page_refs = rest[:4 * pages]
    ogd_ref, ogm_ref, md, ld, accd, mm, lm, accm = rest[4 * pages:]
    p = pl.program_id(1)
    page_len = kn_ref.shape[0]

    def update(m_ref, l_ref, acc_ref, idx, s, vals, mask):
        if mask is not None:
            s = jnp.where(mask, s, -jnp.inf)
        m_prev = m_ref[idx]
        m_new = jnp.maximum(m_prev, jnp.max(s, axis=1, keepdims=True))
        alpha = jnp.exp(m_prev - m_new)
        pr = jnp.exp(s - m_new)
        l_ref[idx] = alpha * l_ref[idx] + jnp.sum(pr, axis=1, keepdims=True)
        pv = _dot(pr[:, 0:page_len], vals[0])
        for g in range(1, len(vals)):
            pv = pv + _dot(pr[:, page_len * g:page_len * (g + 1)], vals[g])
        acc_ref[idx] = alpha * acc_ref[idx] + pv
        m_ref[idx] = m_new

    def attend(ks, vs, cs, rs, mask_d, mask_m):
        cat = lambda xs: xs[0] if len(xs) == 1 else jnp.concatenate(xs, axis=1)
        for h in range(DIFF_HEADS):
            hs = slice(LANES * h, LANES * (h + 1))
            s = cat([_dot_nt(qd_ref[h], kk[:, hs]) for kk in ks])
            update(md, ld, accd, h, s, [vv[:, hs] for vv in vs], mask_d)
        cvals = [cc[...] for cc in cs]
        s = cat([_dot_nt(ql_ref[...], cv) + _dot_nt(qr_ref[...], rr[...]) for cv, rr in zip(cvals, rs)])
        update(mm, lm, accm, 0, s, cvals, mask_m)

    @pl.when(p == 0)
    def _():
        md[...] = jnp.full(md.shape, -jnp.inf, F32)
        ld[...] = jnp.zeros(ld.shape, F32)
        accd[...] = jnp.zeros(accd.shape, F32)
        mm[...] = jnp.full(mm.shape, -jnp.inf, F32)
        lm[...] = jnp.zeros(lm.shape, F32)
        accm[...] = jnp.zeros(accm.shape, F32)

        def causal(rows):
            t_row = lax.broadcasted_iota(jnp.int32, (rows, page_len), 0) % tp
            key = lax.broadcasted_iota(jnp.int32, (rows, page_len), 1)
            return (key <= t_row) & (key < n_new)

        attend([kn_ref], [vn_ref], [cn_ref], [rn_ref], causal(2 * tp), causal(MLA_HEADS * tp))

    attend([page_refs[4 * g] for g in range(pages)], [page_refs[4 * g + 1] for g in range(pages)],
           [page_refs[4 * g + 2] for g in range(pages)], [page_refs[4 * g + 3] for g in range(pages)], None, None)

    @pl.when(p == pl.num_programs(1) - 1)
    def _():
        gate = _silu(gate_ref[...])
        lp = lamp_ref[...]
        lam = (jnp.exp(jnp.sum(lp[0:1] * lp[1:2], axis=1, keepdims=True))
               - jnp.exp(jnp.sum(lp[2:3] * lp[3:4], axis=1, keepdims=True)) + lam_init)
        lat = accm[0] / lm[0]
        for h in range(DIFF_HEADS):
            av = accd[h] / ld[h]
            d = av[0:tp] - lam * av[tp:2 * tp]
            o = (_rms(d) * subln_ref[...]) * (1.0 - lam_init)
            ogd_ref[:, LANES * h:LANES * (h + 1)] = (o * gate[:, LANES * h:LANES * (h + 1)])[0:n_new]
        for h in range(MLA_HEADS):
            o = _dot(lat[tp * h:tp * (h + 1)], wuv_ref[h])
            c0 = DIFF_HEADS * DIFF_VDIM + MLA_VDIM * h
            ogm_ref[:, MLA_VDIM * h:MLA_VDIM * (h + 1)] = (o * gate[:, c0:c0 + MLA_VDIM])[0:n_new]


def _paged_call(page_table, qd, ql, qr, kn, vn, cn, rn, gate, lamp, subln, wuv, pools, *, li, n_new, lam_init):
    bsz, n_pages = page_table.shape
    tp = gate.shape[1]
    pages = math.gcd(PAGES_PER_STEP, n_pages)
    steps = n_pages // pages
    page_len = pools[0].shape[2]
    kern = functools.partial(_paged_kernel, n_new=n_new, tp=tp, pages=pages, lam_init=lam_init)

    def seq_spec(a):
        return pl.BlockSpec((None,) + a.shape[1:], lambda b, p, pt, nd=a.ndim: (b,) + (0,) * (nd - 1))

    def const_spec(a):
        return pl.BlockSpec(a.shape, lambda b, p, pt, nd=a.ndim: (0,) * nd)

    def page_spec(pool, g):
        return pl.BlockSpec((None, None, page_len, pool.shape[3]),
                            lambda b, p, pt, g=g: (li, pt[b * n_pages + p * pages + g], 0, 0))

    page_specs, page_args = [], []
    for g in range(pages):
        for pool in pools:
            page_specs.append(page_spec(pool, g))
            page_args.append(pool)
    out_d = jax.ShapeDtypeStruct((bsz, n_new, DIFF_HEADS * DIFF_VDIM), F32)
    out_m = jax.ShapeDtypeStruct((bsz, n_new, MLA_HEADS * MLA_VDIM), F32)
    out_spec = lambda s: pl.BlockSpec((None,) + s.shape[1:], lambda b, p, pt: (b, 0, 0))
    grid_spec = pltpu.PrefetchScalarGridSpec(
        num_scalar_prefetch=1,
        grid=(bsz, steps),
        in_specs=[seq_spec(a) for a in (qd, ql, qr, kn, vn, cn, rn, gate)]
        + [const_spec(a) for a in (lamp, subln, wuv)] + page_specs,
        out_specs=[out_spec(out_d), out_spec(out_m)],
        scratch_shapes=[pltpu.VMEM((DIFF_HEADS, 2 * tp, 1), F32), pltpu.VMEM((DIFF_HEADS, 2 * tp, 1), F32),
                        pltpu.VMEM((DIFF_HEADS, 2 * tp, DIFF_VDIM), F32),
                        pltpu.VMEM((1, MLA_HEADS * tp, 1), F32), pltpu.VMEM((1, MLA_HEADS * tp, 1), F32),
                        pltpu.VMEM((1, MLA_HEADS * tp, cn.shape[2]), F32)],
    )
    return pl.pallas_call(
        kern,
        grid_spec=grid_spec,
        out_shape=[out_d, out_m],
        compiler_params=pltpu.CompilerParams(dimension_semantics=("arbitrary", "arbitrary"),
                                             vmem_limit_bytes=VMEM_LIMIT),
        name="paged_attention",
    )(page_table.reshape(-1), qd, ql, qr, kn, vn, cn, rn, gate, lamp, subln, wuv, *page_args)


def _lru_front(x, ogd, ogm, wd_ref, wm_ref, norm_ref, win_ref):
    x1 = x + _dot(ogd.astype(BF16), wd_ref[...]) + _dot(ogm.astype(BF16), wm_ref[...])
    hn = (_rms(x1) * norm_ref[...]).astype(BF16)
    return x1, _dot(hn, win_ref[...])


def _lru_gates(conv, blk, wgx_ref, bgx_ref, wga_ref, bga_ref, lam_ref):
    w = wgx_ref.shape[1]
    cs = slice(w * blk, w * (blk + 1))
    c = conv[:, cs]
    cb = c.astype(BF16)
    gx = jax.nn.sigmoid(_dot(cb, wgx_ref[blk]) + bgx_ref[:, cs])
    ga = jax.nn.sigmoid(_dot(cb, wga_ref[blk]) + bga_ref[:, cs])
    log_a = (LRU_C * ga) * jax.nn.log_sigmoid(lam_ref[:, cs])
    th = jnp.tanh(log_a)
    mult = jnp.sqrt(-2.0 * th / (1.0 - th))
    return jnp.exp(log_a), mult * (gx * c)


def _lru_prompt_kernel(x_ref, ogd_ref, ogm_ref, conv0_ref, h0_ref, wd_ref, wm_ref, norm_ref, win_ref, cw_ref,
                       cbias_ref, wgx_ref, bgx_ref, wga_ref, bga_ref, lam_ref, wout_ref,
                       y_ref, hout_ref, cout_ref, ext_sc, a_sc, b_sc, hc_sc, *, tt):
    width = x_ref.shape[1]
    tail = CONV_WIDTH - 1
    base = SUBLANES - tail

    @pl.when(pl.program_id(1) == 0)
    def _():
        ext_sc[base:SUBLANES, :] = conv0_ref[...]
        hc_sc[...] = jnp.broadcast_to(h0_ref[...], hc_sc.shape)

    x1, u = _lru_front(x_ref[...], ogd_ref[...], ogm_ref[...], wd_ref, wm_ref, norm_ref, win_ref)
    y_ref[...] = x1
    ext_sc[SUBLANES:SUBLANES + tt, :] = u[:, 0:width]
    acc = ext_sc[base:base + tt, :] * cw_ref[0:1, :]
    for j in range(1, CONV_WIDTH):
        acc = acc + ext_sc[base + j:base + j + tt, :] * cw_ref[j:j + 1, :]
    conv = cbias_ref[...] + acc
    new_tail = ext_sc[base + tt:SUBLANES + tt, :]
    ext_sc[base:SUBLANES, :] = new_tail
    cout_ref[...] = new_tail

    n_blk = wgx_ref.shape[0]
    bw = wgx_ref.shape[1]
    sub = lax.broadcasted_iota(jnp.int32, (tt // SUBLANES, SUBLANES, bw), 1)
    for blk in range(n_blk):
        a, b = _lru_gates(conv, blk, wgx_ref, bgx_ref, wga_ref, bga_ref, lam_ref)
        a = a.reshape(tt // SUBLANES, SUBLANES, bw)
        b = b.reshape(tt // SUBLANES, SUBLANES, bw)
        d = 1
        while d < SUBLANES:
            keep = sub >= d
            a_sh = jnp.where(keep, pltpu.roll(a, d, 1), 1.0)
            b_sh = jnp.where(keep, pltpu.roll(b, d, 1), 0.0)
            b = a * b_sh + b
            a = a * a_sh
            d *= 2
        a_sc[:, bw * blk:bw * (blk + 1)] = a.reshape(tt, bw)
        b_sc[:, bw * blk:bw * (blk + 1)] = b.reshape(tt, bw)

    def group(g, hc):
        rows = pl.ds(pl.multiple_of(g * SUBLANES, SUBLANES), SUBLANES)
        hs = a_sc[rows, :] * hc + b_sc[rows, :]
        b_sc[rows, :] = hs
        return jnp.broadcast_to(hs[SUBLANES - 1:SUBLANES, :], hs.shape)

    hc = lax.fori_loop(0, tt // SUBLANES, group, hc_sc[...], unroll=4)
    hc_sc[...] = hc
    hout_ref[...] = hc[0:1, :]
    hs = b_sc[...]
    y_ref[...] += _dot((hs * _silu(u[:, width:2 * width])).astype(BF16), wout_ref[...])


def _lru_sample_kernel(x_ref, ogd_ref, ogm_ref, conv0_ref, h0_ref, wd_ref, wm_ref, norm_ref, win_ref, cw_ref,
                       cbias_ref, wgx_ref, bgx_ref, wga_ref, bga_ref, lam_ref, wout_ref,
                       y_ref, hout_ref, cout_ref, *, n_steps):
    width = x_ref.shape[1]
    bsz = x_ref.shape[0] // n_steps
    tail = CONV_WIDTH - 1
    x1, u = _lru_front(x_ref[...], ogd_ref[...], ogm_ref[...], wd_ref, wm_ref, norm_ref, win_ref)
    ext = [conv0_ref[j] for j in range(tail)] + [u[bsz * t:bsz * (t + 1), 0:width] for t in range(n_steps)]
    convs = []
    for t in range(n_steps):
        acc = ext[t] * cw_ref[0:1, :]
        for j in range(1, CONV_WIDTH):
            acc = acc + ext[t + j] * cw_ref[j:j + 1, :]
        convs.append(cbias_ref[...] + acc)
    conv = jnp.concatenate(convs, axis=0)
    for j in range(tail):
        cout_ref[j] = ext[n_steps + j]
    ab = [_lru_gates(conv, blk, wgx_ref, bgx_ref, wga_ref, bga_ref, lam_ref) for blk in range(wgx_ref.shape[0])]
    a = jnp.concatenate([p[0] for p in ab], axis=1)
    b = jnp.concatenate([p[1] for p in ab], axis=1)
    h = h0_ref[...]
    hs = []
    for t in range(n_steps):
        h = a[bsz * t:bsz * (t + 1)] * h + b[bsz * t:bsz * (t + 1)]
        hs.append(h)
    hout_ref[...] = h
    hs = jnp.concatenate(hs, axis=0)
    y_ref[...] = x1 + _dot((hs * _silu(u[:, width:2 * width])).astype(BF16), wout_ref[...])


def _lru_weight_specs(weights, index):
    return [pl.BlockSpec(w.shape, functools.partial(index, nd=w.ndim)) for w in weights]


def _lru_prompt_call(x2d, ogd, ogm, conv0, h0, weights, *, batch, seq):
    n, width = x2d.shape
    tt = min(TT_LRU, seq)
    assert seq % tt == 0
    nt = seq // tt
    tail = CONV_WIDTH - 1
    rows = lambda c: pl.BlockSpec((tt, c), lambda b, t: (b * nt + t, 0))
    state = lambda r: pl.BlockSpec((None, r, width), lambda b, t: (b, 0, 0))
    return pl.pallas_call(
        functools.partial(_lru_prompt_kernel, tt=tt),
        grid=(batch, nt),
        in_specs=[rows(width), rows(ogd.shape[1]), rows(ogm.shape[1]), state(tail), state(1)]
        + _lru_weight_specs(weights, lambda b, t, nd: (0,) * nd),
        out_specs=[rows(width), state(1), state(tail)],
        out_shape=[jax.ShapeDtypeStruct((n, width), F32), jax.ShapeDtypeStruct((batch, 1, width), F32),
                   jax.ShapeDtypeStruct((batch, tail, width), F32)],
        scratch_shapes=[pltpu.VMEM((SUBLANES + tt, width), F32), pltpu.VMEM((tt, width), F32),
                        pltpu.VMEM((tt, width), F32), pltpu.VMEM((SUBLANES, width), F32)],
        compiler_params=pltpu.CompilerParams(dimension_semantics=("arbitrary", "arbitrary"),
                                             vmem_limit_bytes=VMEM_LIMIT),
        name="lru_prompt",
    )(x2d, ogd, ogm, conv0, h0.reshape(batch, 1, width), *weights)


def _lru_sample_call(x_tm, ogd_tm, ogm_tm, conv0_tm, h0, weights, *, n_steps):
    n, width = x_tm.shape
    bsz = n // n_steps
    tail = CONV_WIDTH - 1
    full = lambda a: pl.BlockSpec(a.shape, lambda i, nd=a.ndim: (0,) * nd)
    ins = (x_tm, ogd_tm, ogm_tm, conv0_tm, h0) + tuple(weights)
    outs = [jax.ShapeDtypeStruct((n, width), F32), jax.ShapeDtypeStruct((bsz, width), F32),
            jax.ShapeDtypeStruct((tail, bsz, width), F32)]
    return pl.pallas_call(
        functools.partial(_lru_sample_kernel, n_steps=n_steps),
        grid=(1,),
        in_specs=[full(a) for a in ins],
        out_specs=[full(o) for o in outs],
        out_shape=outs,
        compiler_params=pltpu.CompilerParams(dimension_semantics=("arbitrary",), vmem_limit_bytes=VMEM_LIMIT),
        name="lru_sample",
    )(*ins)


def _rope_rows():
    masks = np.zeros((SUBLANES, LANES), np.float32)
    hd, hm = DIFF_ROT // 2, MLA_ROPE // 2
    for blk in range(LANES // DIFF_DIM):
        o = blk * DIFF_DIM
        masks[1, o + hd:o + 2 * hd] = 1.0
        masks[2, o:o + hd] = -1.0
    masks[4, MLA_NOPE + hm:MLA_NOPE + 2 * hm] = 1.0
    masks[5, MLA_NOPE:MLA_NOPE + hm] = -1.0
    inv_d = ROPE_THETA ** (-jnp.arange(0, DIFF_ROT, 2, dtype=F32) / DIFF_ROT)
    inv_m = ROPE_THETA ** (-jnp.arange(0, MLA_ROPE, 2, dtype=F32) / MLA_ROPE)
    freq_d = jnp.tile(jnp.concatenate([inv_d, inv_d, jnp.zeros((DIFF_DIM - DIFF_ROT,), F32)]), LANES // DIFF_DIM)
    freq_m = jnp.concatenate([jnp.zeros((MLA_NOPE,), F32), inv_m, inv_m,
                              jnp.zeros((LANES - MLA_NOPE - MLA_ROPE,), F32)])
    return jnp.asarray(masks).at[0].set(freq_d).at[3].set(freq_m)


def _seg_matrix(seg):
    idx = np.arange(MXU_DIM) // seg
    return jnp.asarray((idx[:, None] == idx[None, :]).astype(np.float32), dtype=BF16)


def _even_params(norm, w_in, diff_q_norm, diff_k_norm, q_a_norm, w_uq, mla_q_norm, kv_norm, k_rope_norm, w_uk, w_uv):
    d_model = w_in.shape[0]
    q_lora, kv_lora = w_uq.shape[0], w_uk.shape[0]
    head = MLA_NOPE + MLA_ROPE
    pad = LANES - head
    c_kr = 3 * 512 + q_lora + kv_lora
    w = jnp.concatenate([w_in[:, :c_kr], w_in[:, c_kr + MLA_ROPE:], jnp.zeros((d_model, MLA_NOPE), F32),
                         w_in[:, c_kr:c_kr + MLA_ROPE], jnp.zeros((d_model, pad), F32)], axis=1).astype(BF16)
    gains = jnp.zeros((SUBLANES, 2 * LANES), F32)
    gains = gains.at[0, :LANES].set(jnp.tile(diff_q_norm, LANES // DIFF_DIM) * DIFF_SCALE)
    gains = gains.at[1, :LANES].set(jnp.tile(diff_k_norm, LANES // DIFF_DIM))
    gains = gains.at[2, :head].set(mla_q_norm * MLA_SCALE)
    gains = gains.at[3, MLA_NOPE:head].set(k_rope_norm)
    gains = gains.at[4, :q_lora].set(q_a_norm)
    gains = gains.at[5, :kv_lora].set(kv_norm)
    wuq = jnp.pad(w_uq.reshape(q_lora, MLA_HEADS, head), ((0, 0), (0, 0), (0, pad))).reshape(q_lora, -1).astype(BF16)
    w_knope = jnp.pad(w_uk, ((0, 0), (0, 0), (0, LANES - MLA_NOPE))).reshape(kv_lora, -1)
    w_kv = jnp.concatenate([w_knope, w_uv.reshape(kv_lora, -1)], axis=1).astype(BF16)
    w_lat = jnp.pad(jnp.transpose(w_uk, (1, 2, 0)), ((0, 0), (0, LANES - MLA_NOPE), (0, 0))).astype(BF16)
    common = (norm.reshape(1, -1), w, _rope_rows(), gains, _seg_matrix(DIFF_DIM), _seg_matrix(LANES), wuq)
    return common + (w_kv,), common + (w_lat,)


def _lam_rows(lq1, lk1, lq2, lk2):
    rows = jnp.zeros((SUBLANES, LANES), F32)
    for r, v in enumerate((lq1, lk1, lq2, lk2)):
        rows = rows.at[r, :v.shape[0]].set(v)
    return rows


def _even_layer(xp, xs, pools, page_table, li, lam_init, p):
    bp, sp, d = xp.shape
    bs, ts, _ = xs.shape
    page_len = pools[0].shape[2]
    past_len = page_table.shape[1] * page_len
    par_p, par_s = _even_params(p['norm'], p['w_in'], p['diff_q_norm'], p['diff_k_norm'], p['q_a_norm'], p['w_uq'],
                                p['mla_q_norm'], p['kv_norm'], p['k_rope_norm'], p['w_uk'], p['w_uv'])
    lamp = _lam_rows(p['lq1'], p['lk1'], p['lq2'], p['lk2'])
    subln = p['subln'].reshape(1, -1)

    (dq, dk, dv, dkb, dvb, qm, ckv, kr, km, vm, gate) = _even_in_call(
        xp.reshape(bp * sp, d), par_p, prompt=True, pos_base=0, pos_period=sp, pos_div=False)
    ogd_p = _flash_call(dq, dkb, dvb, gate, lamp, subln, batch=bp, seq=sp, n_maps=2, gate_col0=0,
                        lam_init=lam_init, name="flash_diff")
    ogm_p = _flash_call(qm, km, vm, gate, lamp, subln, batch=bp, seq=sp, n_maps=1, gate_col0=DIFF_HEADS,
                        lam_init=lam_init, name="flash_mla")
    rows_p = (dk.reshape(bp, sp, DIFF_HEADS, 2, DIFF_DIM), dv.reshape(bp, sp, DIFF_HEADS, DIFF_VDIM),
              ckv.reshape(bp, sp, -1), kr.reshape(bp, sp, -1))

    (sdq, sdk, sdv, sqlat, sqrope, sckv, skr, sgate) = _even_in_call(
        xs.reshape(bs * ts, d), par_s, prompt=False, pos_base=past_len, pos_period=ts, pos_div=False)
    tp = _round_up(ts, SUBLANES)
    padt = lambda a, n: jnp.pad(a, ((0, 0), (0, n - a.shape[1])) + ((0, 0),) * (a.ndim - 2))
    q5 = sdq.reshape(bs, ts, DIFF_HEADS, 2, DIFF_DIM)
    keep0 = jnp.array([1.0, 0.0], F32).reshape(1, 1, 1, 2, 1)
    qd = jnp.stack([padt(q5 * keep0, tp), padt(q5 * (1.0 - keep0), tp)], axis=1)
    qd = jnp.transpose(qd.reshape(bs, 2, tp, DIFF_HEADS, LANES), (0, 3, 1, 2, 4)).reshape(bs, DIFF_HEADS, 2 * tp, LANES)
    kv_lora = sckv.shape[1]
    ql = jnp.transpose(padt(sqlat.reshape(bs, ts, MLA_HEADS, kv_lora), tp), (0, 2, 1, 3)).reshape(bs, MLA_HEADS * tp, kv_lora)
    qr = jnp.transpose(padt(sqrope.reshape(bs, ts, MLA_HEADS, MLA_ROPE), tp), (0, 2, 1, 3)).reshape(bs, MLA_HEADS * tp, MLA_ROPE)
    new = lambda a: padt(a.reshape(bs, ts, -1), page_len)
    ogd_s, ogm_s = _paged_call(page_table, qd, ql, qr, new(sdk), new(sdv), new(sckv), new(skr),
                               padt(sgate.reshape(bs, ts, -1), tp), lamp, subln,
                               jnp.transpose(p['w_uv'], (1, 0, 2)), pools, li=li, n_new=ts, lam_init=lam_init)
    rows_s = (sdk.reshape(bs, ts, DIFF_HEADS, 2, DIFF_DIM), sdv.reshape(bs, ts, DIFF_HEADS, DIFF_VDIM),
              sckv.reshape(bs, ts, -1), skr.reshape(bs, ts, -1))
    return (ogd_p, ogm_p), (ogd_s, ogm_s), rows_p, rows_s


def _odd_layer(xp, xs, og_p, og_s, w_out_even, state_h, state_conv, p):
    bp, sp, d = xp.shape
    bs, ts, _ = xs.shape
    n_diff = DIFF_HEADS * DIFF_VDIM
    width = p['w_out'].shape[0]
    weights = (w_out_even[:n_diff].astype(BF16), w_out_even[n_diff:].astype(BF16), p['norm'].reshape(1, -1),
               p['w_in'].astype(BF16), p['conv_w'], p['conv_b'].reshape(1, -1), p['w_gx'].astype(BF16),
               p['b_gx'].reshape(1, -1), p['w_ga'].astype(BF16), p['b_ga'].reshape(1, -1), p['lam'].reshape(1, -1),
               p['w_out'].astype(BF16))
    tail = CONV_WIDTH - 1
    yp, hp, cp = _lru_prompt_call(xp.reshape(bp * sp, d), og_p[0], og_p[1], jnp.zeros((bp, tail, width), F32),
                                  jnp.zeros((bp, width), F32), weights, batch=bp, seq=sp)
    tmaj = lambda a: jnp.transpose(a, (1, 0, 2)).reshape(ts * bs, a.shape[2])
    ys, hs, cs = _lru_sample_call(tmaj(xs), tmaj(og_s[0]), tmaj(og_s[1]), jnp.transpose(state_conv, (1, 0, 2)),
                                  state_h, weights, n_steps=ts)
    ys = jnp.transpose(ys.reshape(ts, bs, d), (1, 0, 2))
    return yp.reshape(bp, sp, d), ys, (hp.reshape(bp, width), cp), (hs, jnp.transpose(cs, (1, 0, 2)))


def kernel(x_prompt, x_sample, cache_diff_k, cache_diff_v, cache_mla_ckv, cache_mla_krope, state_lru_h, state_lru_conv, page_table, norm_even, w_in_even, diff_q_norm, diff_k_norm, diff_lambda_q1, diff_lambda_k1, diff_lambda_q2, diff_lambda_k2, diff_subln, mla_q_a_norm, mla_w_uq, mla_q_norm, mla_kv_norm, mla_k_rope_norm, mla_w_uk, mla_w_uv, w_out_even, norm_odd, w_in_odd, lru_conv_w, lru_conv_b, lru_w_gx, lru_b_gx, lru_w_ga, lru_b_ga, lru_lambda, w_out_odd):
    n_even, n_odd = norm_even.shape[0], norm_odd.shape[0]
    assert n_even == n_odd, "layers come in (even, odd) pairs"
    n_pool, page_len = cache_diff_k.shape[1], cache_diff_k.shape[2]
    pools = (cache_diff_k.reshape(n_even, n_pool, page_len, -1), cache_diff_v.reshape(n_even, n_pool, page_len, -1),
             cache_mla_ckv, cache_mla_krope)
    yp, ys = x_prompt, x_sample
    rows_p, rows_s, lru_p, lru_s = [], [], [], []
    for li in range(n_even):
        layer = 2 * li
        lam_init = 0.8 - 0.6 * math.exp(-0.3 * layer)
        pe = {'norm': norm_even[li], 'w_in': w_in_even[li], 'diff_q_norm': diff_q_norm[li],
              'diff_k_norm': diff_k_norm[li], 'lq1': diff_lambda_q1[li], 'lk1': diff_lambda_k1[li],
              'lq2': diff_lambda_q2[li], 'lk2': diff_lambda_k2[li], 'subln': diff_subln[li],
              'q_a_norm': mla_q_a_norm[li], 'w_uq': mla_w_uq[li], 'mla_q_norm': mla_q_norm[li],
              'kv_norm': mla_kv_norm[li], 'k_rope_norm': mla_k_rope_norm[li], 'w_uk': mla_w_uk[li],
              'w_uv': mla_w_uv[li]}
        og_p, og_s, r_p, r_s = _even_layer(yp, ys, pools, page_table, li, lam_init, pe)
        rows_p.append(r_p)
        rows_s.append(r_s)
        po = {'norm': norm_odd[li], 'w_in': w_in_odd[li], 'conv_w': lru_conv_w[li], 'conv_b': lru_conv_b[li],
              'w_gx': lru_w_gx[li], 'b_gx': lru_b_gx[li], 'w_ga': lru_w_ga[li], 'b_ga': lru_b_ga[li],
              'lam': lru_lambda[li], 'w_out': w_out_odd[li]}
        yp, ys, l_p, l_s = _odd_layer(yp, ys, og_p, og_s, w_out_even[li], state_lru_h[li], state_lru_conv[li], po)
        lru_p.append(l_p)
        lru_s.append(l_s)
    stack = lambda rows, k: jnp.stack([r[k] for r in rows])
    return (yp, ys, stack(rows_p, 0), stack(rows_p, 1), stack(rows_p, 2), stack(rows_p, 3), stack(lru_p, 0),
            stack(lru_p, 1), stack(rows_s, 0), stack(rows_s, 1), stack(rows_s, 2), stack(rows_s, 3), stack(lru_s, 0),
            stack(lru_s, 1))
```

```python
import functools
import math

import numpy as np
import jax
import jax.numpy as jnp
from jax import lax
from jax.experimental import pallas as pl
from jax.experimental.pallas import tpu as pltpu

DIFF_HEADS = 4
DIFF_DIM = 64
DIFF_VDIM = 2 * DIFF_DIM
DIFF_ROT = DIFF_DIM // 4
DIFF_SCALE = DIFF_DIM ** -0.5
MLA_HEADS = 4
MLA_NOPE = 64
MLA_ROPE = 32
MLA_VDIM = 128
MLA_SCALE = (MLA_NOPE + MLA_ROPE) ** -0.5
CONV_WIDTH = 4
LRU_C = 8.0
ROPE_THETA = 500000.0
EPS = 1e-6
LOG2E = math.log2(math.e)

LANES = 128
SUBLANES = 8
MXU_DIM = 256
VMEM_LIMIT = 52 * 1024 * 1024

TM_IN = 256
TQ_ATTN = 512
TT_LRU = 256
PAGES_PER_STEP = 8

F32 = jnp.float32
BF16 = jnp.bfloat16


def _round_up(n, m):
    return (n + m - 1) // m * m


def _dot(a, b):
    return jnp.dot(a, b, preferred_element_type=F32)


def _dot_nt(a, b):
    return lax.dot_general(a, b, (((1,), (1,)), ((), ())), preferred_element_type=F32)


def _rms(x, inv_n=None):
    ms = jnp.mean(x * x, axis=-1, keepdims=True) if inv_n is None else jnp.sum(x * x, axis=-1, keepdims=True) * inv_n
    return x * lax.rsqrt(ms + EPS)


def _silu(x):
    return x * jax.nn.sigmoid(x)


def _even_in_kernel(x_ref, norm_ref, w_ref, rope_ref, gains_ref, seg64_ref, seg128_ref, wuq_ref, wx_ref,
                    *out_refs, prompt, tm, pos_base, pos_period, pos_div):
    x = x_ref[...]
    h = (_rms(x) * norm_ref[...]).astype(BF16)
    u = _dot(h, w_ref[...])

    row = pl.program_id(0) * tm + lax.broadcasted_iota(jnp.int32, (tm, LANES), 0)
    pos = (row // pos_period if pos_div else row % pos_period) + pos_base
    posf = pos.astype(F32)

    def tables(k):
        ang = posf * rope_ref[3 * k:3 * k + 1, :]
        s = jnp.sin(ang)
        return jnp.cos(ang), s * rope_ref[3 * k + 1:3 * k + 2, :], s * rope_ref[3 * k + 2:3 * k + 3, :]

    def rope(v, tabs, half):
        c, sa, sb = tabs
        return v * c + pltpu.roll(v, half, 1) * sa + pltpu.roll(v, LANES - half, 1) * sb

    def seg_sum(v, seg_ref):
        sq = v * v
        hi = sq.astype(BF16)
        lo = (sq - hi.astype(F32)).astype(BF16)
        return _dot(hi, seg_ref[...]) + _dot(lo, seg_ref[...])

    def head_norm_rope(v, seg_ref, inv_n, gain, tabs, half):
        slabs = []
        for c in range(v.shape[1] // MXU_DIM):
            vc = v[:, MXU_DIM * c:MXU_DIM * (c + 1)]
            y = vc * lax.rsqrt(seg_sum(vc, seg_ref) * inv_n + EPS)
            for s in range(MXU_DIM // LANES):
                slabs.append(rope(y[:, LANES * s:LANES * (s + 1)] * gain, tabs, half))
        return slabs

    tab_d = tables(0)
    tab_m = tables(1)
    dq = head_norm_rope(u[:, 0:512], seg64_ref, 1.0 / DIFF_DIM, gains_ref[0:1, 0:LANES], tab_d, DIFF_ROT // 2)
    dk = head_norm_rope(u[:, 512:1024], seg64_ref, 1.0 / DIFF_DIM, gains_ref[1:2, 0:LANES], tab_d, DIFF_ROT // 2)
    dv = u[:, 1024:1536]
    cqn = _rms(u[:, 1536:1792]) * gains_ref[4:5, :]
    q = _dot(cqn.astype(BF16), wuq_ref[...])
    qm = head_norm_rope(q, seg128_ref, 1.0 / (MLA_NOPE + MLA_ROPE), gains_ref[2:3, 0:LANES], tab_m, MLA_ROPE // 2)
    ckv = _rms(u[:, 1792:2048]) * gains_ref[5:6, :]
    gate = u[:, 2048:3072]
    kr_slab = rope(_rms(u[:, 3072:3200], 1.0 / MLA_ROPE) * gains_ref[3:4, 0:LANES], tab_m, MLA_ROPE // 2)
    kr = pltpu.roll(kr_slab, LANES - MLA_NOPE, 1)[:, 0:MLA_ROPE]

    def put(ref, slabs):
        for k, s in enumerate(slabs):
            ref[:, LANES * k:LANES * (k + 1)] = s.astype(ref.dtype)

    if prompt:
        dq_o, dk_o, dv_o, dkb_o, dvt_o, qm_o, ckv_o, kr_o, km_o, vmt_o, gate_o = out_refs
        put(dq_o, dq)
        put(dk_o, dk)
        put(dkb_o, dk)
        dv_o[...] = dv
        dvt_o[...] = dv.T.astype(BF16)
        put(qm_o, qm)
        ckv_o[...] = ckv
        kr_o[...] = kr
        kv = _dot(ckv.astype(BF16), wx_ref[...])
        put(km_o, [kv[:, LANES * hh:LANES * (hh + 1)] + kr_slab for hh in range(MLA_HEADS)])
        vmt_o[...] = kv[:, 512:1024].T.astype(BF16)
        gate_o[...] = gate
    else:
        dq_o, dk_o, dv_o, qlat_o, qrope_o, ckv_o, kr_o, gate_o = out_refs
        put(dq_o, dq)
        put(dk_o, dk)
        dv_o[...] = dv
        ckv_o[...] = ckv
        kr_o[...] = kr
        gate_o[...] = gate
        lane = lax.broadcasted_iota(jnp.int32, (tm, LANES), 1)
        qr = jnp.zeros((tm, LANES), F32)
        for hh in range(MLA_HEADS):
            qlat_o[:, MXU_DIM * hh:MXU_DIM * (hh + 1)] = _dot(qm[hh].astype(BF16), wx_ref[hh])
            moved = pltpu.roll(qm[hh], (MLA_ROPE * hh + LANES - MLA_NOPE) % LANES, 1)
            qr = qr + jnp.where((lane >= MLA_ROPE * hh) & (lane < MLA_ROPE * (hh + 1)), moved, 0.0)
        qrope_o[...] = qr


def _even_in_call(x2d, params, *, prompt, seq, pos_base, pos_period, pos_div):
    n, d = x2d.shape
    tm = min(TM_IN, seq)
    assert seq % tm == 0 and n % seq == 0
    nt = seq // tm
    (norm, w, rope_c, gains, seg64, seg128, wuq, wx) = params
    full = lambda a: pl.BlockSpec(a.shape, lambda i, nd=a.ndim: (0,) * nd)
    tile = lambda c: pl.BlockSpec((tm, c), lambda i: (i, 0))
    tile_t = pl.BlockSpec((512, tm), lambda i: (i // nt, i % nt))
    shape_t = jax.ShapeDtypeStruct((n // seq * 512, seq), BF16)
    if prompt:
        outs = [(512, BF16), (512, F32), (512, F32), (512, BF16), None, (512, BF16), (256, F32),
                (MLA_ROPE, F32), (512, BF16), None, (1024, F32)]
    else:
        outs = [(512, F32), (512, F32), (512, F32), (1024, F32), (LANES, F32), (256, F32), (MLA_ROPE, F32),
                (1024, F32)]
    kern = functools.partial(_even_in_kernel, prompt=prompt, tm=tm, pos_base=pos_base, pos_period=pos_period,
                             pos_div=pos_div)
    return pl.pallas_call(
        kern,
        grid=(n // tm,),
        in_specs=[tile(d), full(norm), full(w), full(rope_c), full(gains), full(seg64), full(seg128), full(wuq),
                  full(wx)],
        out_specs=[tile_t if o is None else tile(o[0]) for o in outs],
        out_shape=[shape_t if o is None else jax.ShapeDtypeStruct((n, o[0]), o[1]) for o in outs],
        compiler_params=pltpu.CompilerParams(dimension_semantics=("arbitrary",), vmem_limit_bytes=VMEM_LIMIT),
        name="even_in_prompt" if prompt else "even_in_sample",
    )(x2d, norm, w, rope_c, gains, seg64, seg128, wuq, wx)


def _flash_kernel(q_ref, k_ref, vt_ref, gate_ref, lamp_ref, subln_ref, o_ref, m_sc, l_sc, acc_sc, *,
                  n_maps, tq, lam_init):
    i = pl.program_id(2)
    q = q_ref[...]
    if n_maps == 2:
        lane = lax.broadcasted_iota(jnp.int32, q.shape, 1)
        qf = q.astype(F32)
        qs = [jnp.where(lane < DIFF_DIM, qf, 0.0).astype(BF16), jnp.where(lane >= DIFF_DIM, qf, 0.0).astype(BF16)]
    else:
        qs = [q]
    m_sc[...] = jnp.full(m_sc.shape, -jnp.inf, F32)
    l_sc[...] = jnp.zeros(l_sc.shape, F32)
    acc_sc[...] = jnp.zeros(acc_sc.shape, F32)

    def scores(j):
        k = k_ref[pl.ds(pl.multiple_of(j * tq, tq), tq), :]
        return tuple(_dot_nt(k, qs[a]) for a in range(n_maps))

    def update(j, ss, masked):
        vt = vt_ref[:, pl.ds(pl.multiple_of(j * tq, tq), tq)]
        for a in range(n_maps):
            s = ss[a]
            if masked:
                key_id = lax.broadcasted_iota(jnp.int32, s.shape, 0)
                qry_id = lax.broadcasted_iota(jnp.int32, s.shape, 1)
                s = jnp.where(key_id <= qry_id, s, -jnp.inf)
            m_prev = m_sc[a]
            m_new = jnp.maximum(m_prev, jnp.max(s, axis=0, keepdims=True))
            alpha = jnp.exp2(m_prev - m_new)
            p = jnp.exp2(s - m_new)
            l_sc[a] = alpha * l_sc[a] + jnp.sum(p, axis=0, keepdims=True)
            acc_sc[a] = alpha * acc_sc[a] + _dot(vt, p.astype(BF16))
            m_sc[a] = m_new

    def body(j, ss):
        nxt = scores(j + 1)
        update(j, ss, False)
        return nxt

    update(i, lax.fori_loop(0, i, body, scores(0)), True)

    g = _silu(gate_ref[...])
    if n_maps == 2:
        lp = lamp_ref[...]
        lam = (jnp.exp(jnp.sum(lp[0:1] * lp[1:2], axis=1, keepdims=True))
               - jnp.exp(jnp.sum(lp[2:3] * lp[3:4], axis=1, keepdims=True)) + lam_init)
        d = (acc_sc[0] / l_sc[0] - lam * (acc_sc[1] / l_sc[1])).T
        o = (_rms(d) * subln_ref[...]) * (1.0 - lam_init)
    else:
        o = (acc_sc[0] / l_sc[0]).T
    o_ref[...] = (o * g).astype(o_ref.dtype)


def _flash_call(q, k, vt, gate, lamp, subln, *, batch, seq, n_maps, gate_col0, lam_init, name):
    n, width = q.shape
    heads = width // LANES
    tq = min(TQ_ATTN, seq)
    assert seq % tq == 0
    nq = seq // tq
    kern = functools.partial(_flash_kernel, n_maps=n_maps, tq=tq, lam_init=lam_init)
    qspec = pl.BlockSpec((tq, LANES), lambda b, h, i: (b * nq + i, h))
    return pl.pallas_call(
        kern,
        grid=(batch, heads, nq),
        in_specs=[qspec,
                  pl.BlockSpec((seq, LANES), lambda b, h, i: (b, h)),
                  pl.BlockSpec((LANES, seq), lambda b, h, i: (b * heads + h, 0)),
                  pl.BlockSpec((tq, LANES), lambda b, h, i: (b * nq + i, gate_col0 + h)),
                  pl.BlockSpec(lamp.shape, lambda b, h, i: (0, 0)),
                  pl.BlockSpec(subln.shape, lambda b, h, i: (0, 0))],
        out_specs=qspec,
        out_shape=jax.ShapeDtypeStruct((n, width), BF16),
        scratch_shapes=[pltpu.VMEM((n_maps, 1, tq), F32), pltpu.VMEM((n_maps, 1, tq), F32),
                        pltpu.VMEM((n_maps, LANES, tq), F32)],
        compiler_params=pltpu.CompilerParams(dimension_semantics=("arbitrary",) * 3, vmem_limit_bytes=VMEM_LIMIT),
        name=name,
    )(q, k, vt, gate, lamp, subln)


def _paged_kernel(pt_ref, qbd_ref, ql_ref, qr_ref, knt_ref, vn_ref, cn_ref, rnt_ref, gate_ref, lamp_ref, subln_ref,
                  wuv_ref, *rest, n_new, tp, pages, lam_init):
    del pt_ref
    page_refs = rest[:4 * pages]
    ogd_ref, ogm_ref, m_sc, l_sc, accd, accm = rest[4 * pages:]
    p = pl.program_id(1)
    page_len = cn_ref.shape[0]
    rows_h = 2 * tp
    n_d = DIFF_HEADS * rows_h

    def attend(kts, vs, cs, rts, mask):
        cat = lambda xs: xs[0] if len(xs) == 1 else jnp.concatenate(xs, axis=1)
        cvals = [c[...] for c in cs]
        sd = cat([_dot(qbd_ref[...], kt[...]) for kt in kts])
        sm = cat([_dot_nt(ql_ref[...], cv) + _dot(qr_ref[...], rt[...]) for cv, rt in zip(cvals, rts)])
        s = jnp.concatenate([sd, sm], axis=0)
        if mask is not None:
            s = jnp.where(mask, s, -jnp.inf)
        m_prev = m_sc[...]
        m_new = jnp.maximum(m_prev, jnp.max(s, axis=1, keepdims=True))
        alpha = jnp.exp(m_prev - m_new)
        pr = jnp.exp(s - m_new)
        l_sc[...] = alpha * l_sc[...] + jnp.sum(pr, axis=1, keepdims=True)
        m_sc[...] = m_new
        pvs = []
        for h in range(DIFF_HEADS):
            acc = None
            for g, v in enumerate(vs):
                t = _dot(pr[rows_h * h:rows_h * (h + 1), page_len * g:page_len * (g + 1)],
                         v[pl.ds(h, page_len, stride=DIFF_HEADS), :])
                acc = t if acc is None else acc + t
            pvs.append(acc)
        accd[...] = alpha[0:n_d] * accd[...] + jnp.concatenate(pvs, axis=0)
        acc = None
        for g, cv in enumerate(cvals):
            t = _dot(pr[n_d:, page_len * g:page_len * (g + 1)], cv)
            acc = t if acc is None else acc + t
        accm[...] = alpha[n_d:] * accm[...] + acc

    @pl.when(p == 0)
    def _():
        m_sc[...] = jnp.full(m_sc.shape, -jnp.inf, F32)
        l_sc[...] = jnp.zeros(l_sc.shape, F32)
        accd[...] = jnp.zeros(accd.shape, F32)
        accm[...] = jnp.zeros(accm.shape, F32)
        t_row = lax.broadcasted_iota(jnp.int32, (m_sc.shape[0], page_len), 0) % tp
        key = lax.broadcasted_iota(jnp.int32, (m_sc.shape[0], page_len), 1)
        attend([knt_ref], [vn_ref], [cn_ref], [rnt_ref], (key <= t_row) & (key < n_new))

    attend([page_refs[4 * g] for g in range(pages)], [page_refs[4 * g + 1] for g in range(pages)],
           [page_refs[4 * g + 2] for g in range(pages)], [page_refs[4 * g + 3] for g in range(pages)], None)

    @pl.when(p == pl.num_programs(1) - 1)
    def _():
        gate = _silu(gate_ref[...])
        lp = lamp_ref[...]
        lam = (jnp.exp(jnp.sum(lp[0:1] * lp[1:2], axis=1, keepdims=True))
               - jnp.exp(jnp.sum(lp[2:3] * lp[3:4], axis=1, keepdims=True)) + lam_init)
        l = l_sc[...]
        av = accd[...] / l[0:n_d]
        lat = accm[...] / l[n_d:]
        for h in range(DIFF_HEADS):
            d = av[rows_h * h:rows_h * h + tp] - lam * av[rows_h * h + tp:rows_h * (h + 1)]
            o = (_rms(d) * subln_ref[...]) * (1.0 - lam_init)
            ogd_ref[:, LANES * h:LANES * (h + 1)] = (o * gate[:, LANES * h:LANES * (h + 1)])[0:n_new]
        for h in range(MLA_HEADS):
            o = _dot(lat[tp * h:tp * (h + 1)], wuv_ref[h])
            c0 = DIFF_HEADS * DIFF_VDIM + MLA_VDIM * h
            ogm_ref[:, MLA_VDIM * h:MLA_VDIM * (h + 1)] = (o * gate[:, c0:c0 + MLA_VDIM])[0:n_new]


def _paged_call(page_table, qbd, ql, qr, knt, vn, cn, rnt, gate, lamp, subln, wuv, pools, *, li, n_new, lam_init):
    bsz, n_pages = page_table.shape
    tp = gate.shape[1]
    pages = math.gcd(PAGES_PER_STEP, n_pages)
    steps = n_pages // pages
    kern = functools.partial(_paged_kernel, n_new=n_new, tp=tp, pages=pages, lam_init=lam_init)

    def seq_spec(a):
        return pl.BlockSpec((None,) + a.shape[1:], lambda b, p, pt, nd=a.ndim: (b,) + (0,) * (nd - 1))

    def const_spec(a):
        return pl.BlockSpec(a.shape, lambda b, p, pt, nd=a.ndim: (0,) * nd)

    def page_spec(pool, g):
        return pl.BlockSpec((None, None) + pool.shape[2:],
                            lambda b, p, pt, g=g: (li, pt[b * n_pages + p * pages + g], 0, 0))

    page_specs, page_args = [], []
    for g in range(pages):
        for pool in pools:
            page_specs.append(page_spec(pool, g))
            page_args.append(pool)
    out_d = jax.ShapeDtypeStruct((bsz, n_new, DIFF_HEADS * DIFF_VDIM), F32)
    out_m = jax.ShapeDtypeStruct((bsz, n_new, MLA_HEADS * MLA_VDIM), F32)
    out_spec = lambda s: pl.BlockSpec((None,) + s.shape[1:], lambda b, p, pt: (b, 0, 0))
    n_rows = qbd.shape[1] + ql.shape[1]
    grid_spec = pltpu.PrefetchScalarGridSpec(
        num_scalar_prefetch=1,
        grid=(bsz, steps),
        in_specs=[seq_spec(a) for a in (qbd, ql, qr, knt, vn, cn, rnt, gate)]
        + [const_spec(a) for a in (lamp, subln, wuv)] + page_specs,
        out_specs=[out_spec(out_d), out_spec(out_m)],
        scratch_shapes=[pltpu.VMEM((n_rows, 1), F32), pltpu.VMEM((n_rows, 1), F32),
                        pltpu.VMEM((qbd.shape[1], DIFF_VDIM), F32), pltpu.VMEM((ql.shape[1], cn.shape[2]), F32)],
    )
    return pl.pallas_call(
        kern,
        grid_spec=grid_spec,
        out_shape=[out_d, out_m],
        compiler_params=pltpu.CompilerParams(dimension_semantics=("arbitrary", "arbitrary"),
                                             vmem_limit_bytes=VMEM_LIMIT),
        name="paged_attention",
    )(page_table.reshape(-1), qbd, ql, qr, knt, vn, cn, rnt, gate, lamp, subln, wuv, *page_args)


def _lru_front(x, ogd, ogm, wd_ref, wm_ref, norm_ref, win_ref):
    x1 = x + _dot(ogd.astype(BF16), wd_ref[...]) + _dot(ogm.astype(BF16), wm_ref[...])
    hn = (_rms(x1) * norm_ref[...]).astype(BF16)
    return x1, _dot(hn, win_ref[...])


def _lru_gates(conv, blk, wgx_ref, bgx_ref, wga_ref, bga_ref, lam_ref):
    w = wgx_ref.shape[1]
    cs = slice(w * blk, w * (blk + 1))
    c = conv[:, cs]
    cb = c.astype(BF16)
    gx = jax.nn.sigmoid(_dot(cb, wgx_ref[blk]) + bgx_ref[:, cs])
    ga = jax.nn.sigmoid(_dot(cb, wga_ref[blk]) + bga_ref[:, cs])
    log_a = (LRU_C * ga) * jax.nn.log_sigmoid(lam_ref[:, cs])
    th = jnp.tanh(log_a)
    mult = jnp.sqrt(-2.0 * th / (1.0 - th))
    return jnp.exp(log_a), mult * (gx * c)


def _lru_prompt_kernel(x_ref, ogd_ref, ogm_ref, conv0_ref, h0_ref, wd_ref, wm_ref, norm_ref, win_ref, cw_ref,
                       cbias_ref, wgx_ref, bgx_ref, wga_ref, bga_ref, lam_ref, wout_ref,
                       y_ref, hout_ref, cout_ref, ext_sc, a_sc, b_sc, hc_sc, *, tt):
    width = x_ref.shape[1]
    tail = CONV_WIDTH - 1
    base = SUBLANES - tail

    @pl.when(pl.program_id(1) == 0)
    def _():
        ext_sc[base:SUBLANES, :] = conv0_ref[...]
        hc_sc[...] = jnp.broadcast_to(h0_ref[...], hc_sc.shape)

    x1, u = _lru_front(x_ref[...], ogd_ref[...], ogm_ref[...], wd_ref, wm_ref, norm_ref, win_ref)
    y_ref[...] = x1
    ext_sc[SUBLANES:SUBLANES + tt, :] = u[:, 0:width]
    acc = ext_sc[base:base + tt, :] * cw_ref[0:1, :]
    for j in range(1, CONV_WIDTH):
        acc = acc + ext_sc[base + j:base + j + tt, :] * cw_ref[j:j + 1, :]
    conv = cbias_ref[...] + acc
    new_tail = ext_sc[base + tt:SUBLANES + tt, :]
    ext_sc[base:SUBLANES, :] = new_tail
    cout_ref[...] = new_tail

    n_blk = wgx_ref.shape[0]
    bw = wgx_ref.shape[1]
    sub = lax.broadcasted_iota(jnp.int32, (tt // SUBLANES, SUBLANES, bw), 1)
    for blk in range(n_blk):
        a, b = _lru_gates(conv, blk, wgx_ref, bgx_ref, wga_ref, bga_ref, lam_ref)
        a = a.reshape(tt // SUBLANES, SUBLANES, bw)
        b = b.reshape(tt // SUBLANES, SUBLANES, bw)
        d = 1
        while d < SUBLANES:
            keep = sub >= d
            a_sh = jnp.where(keep, pltpu.roll(a, d, 1), 1.0)
            b_sh = jnp.where(keep, pltpu.roll(b, d, 1), 0.0)
            b = a * b_sh + b
            a = a * a_sh
            d *= 2
        a_sc[:, bw * blk:bw * (blk + 1)] = a.reshape(tt, bw)
        b_sc[:, bw * blk:bw * (blk + 1)] = b.reshape(tt, bw)

    def group(g, hc):
        rows = pl.ds(pl.multiple_of(g * SUBLANES, SUBLANES), SUBLANES)
        hs = a_sc[rows, :] * hc + b_sc[rows, :]
        b_sc[rows, :] = hs
        return jnp.broadcast_to(hs[SUBLANES - 1:SUBLANES, :], hs.shape)

    hc = lax.fori_loop(0, tt // SUBLANES, group, hc_sc[...], unroll=4)
    hc_sc[...] = hc
    hout_ref[...] = hc[0:1, :]
    hs = b_sc[...]
    y_ref[...] += _dot((hs * _silu(u[:, width:2 * width])).astype(BF16), wout_ref[...])


def _lru_sample_kernel(x_ref, ogd_ref, ogm_ref, conv0_ref, h0_ref, wd_ref, wm_ref, norm_ref, win_ref, cw_ref,
                       cbias_ref, wgx_ref, bgx_ref, wga_ref, bga_ref, lam_ref, wout_ref,
                       y_ref, hout_ref, cout_ref, *, n_steps):
    width = x_ref.shape[1]
    bsz = x_ref.shape[0] // n_steps
    tail = CONV_WIDTH - 1
    x1, u = _lru_front(x_ref[...], ogd_ref[...], ogm_ref[...], wd_ref, wm_ref, norm_ref, win_ref)
    ext = [conv0_ref[j] for j in range(tail)] + [u[bsz * t:bsz * (t + 1), 0:width] for t in range(n_steps)]
    convs = []
    for t in range(n_steps):
        acc = ext[t] * cw_ref[0:1, :]
        for j in range(1, CONV_WIDTH):
            acc = acc + ext[t + j] * cw_ref[j:j + 1, :]
        convs.append(cbias_ref[...] + acc)
    conv = jnp.concatenate(convs, axis=0)
    for j in range(tail):
        cout_ref[j] = ext[n_steps + j]
    ab = [_lru_gates(conv, blk, wgx_ref, bgx_ref, wga_ref, bga_ref, lam_ref) for blk in range(wgx_ref.shape[0])]
    a = jnp.concatenate([p[0] for p in ab], axis=1)
    b = jnp.concatenate([p[1] for p in ab], axis=1)
    h = h0_ref[...]
    hs = []
    for t in range(n_steps):
        h = a[bsz * t:bsz * (t + 1)] * h + b[bsz * t:bsz * (t + 1)]
        hs.append(h)
    hout_ref[...] = h
    hs = jnp.concatenate(hs, axis=0)
    y_ref[...] = x1 + _dot((hs * _silu(u[:, width:2 * width])).astype(BF16), wout_ref[...])


def _lru_weight_specs(weights, index):
    return [pl.BlockSpec(w.shape, functools.partial(index, nd=w.ndim)) for w in weights]


def _lru_prompt_call(x2d, ogd, ogm, conv0, h0, weights, *, batch, seq):
    n, width = x2d.shape
    tt = min(TT_LRU, seq)
    assert seq % tt == 0
    nt = seq // tt
    tail = CONV_WIDTH - 1
    rows = lambda c: pl.BlockSpec((tt, c), lambda b, t: (b * nt + t, 0))
    state = lambda r: pl.BlockSpec((None, r, width), lambda b, t: (b, 0, 0))
    return pl.pallas_call(
        functools.partial(_lru_prompt_kernel, tt=tt),
        grid=(batch, nt),
        in_specs=[rows(width), rows(ogd.shape[1]), rows(ogm.shape[1]), state(tail), state(1)]
        + _lru_weight_specs(weights, lambda b, t, nd: (0,) * nd),
        out_specs=[rows(width), state(1), state(tail)],
        out_shape=[jax.ShapeDtypeStruct((n, width), F32), jax.ShapeDtypeStruct((batch, 1, width), F32),
                   jax.ShapeDtypeStruct((batch, tail, width), F32)],
        scratch_shapes=[pltpu.VMEM((SUBLANES + tt, width), F32), pltpu.VMEM((tt, width), F32),
                        pltpu.VMEM((tt, width), F32), pltpu.VMEM((SUBLANES, width), F32)],
        compiler_params=pltpu.CompilerParams(dimension_semantics=("arbitrary", "arbitrary"),
                                             vmem_limit_bytes=VMEM_LIMIT),
        name="lru_prompt",
    )(x2d, ogd, ogm, conv0, h0.reshape(batch, 1, width), *weights)


def _lru_sample_call(x_tm, ogd_tm, ogm_tm, conv0_tm, h0, weights, *, n_steps):
    n, width = x_tm.shape
    bsz = n // n_steps
    tail = CONV_WIDTH - 1
    full = lambda a: pl.BlockSpec(a.shape, lambda i, nd=a.ndim: (0,) * nd)
    ins = (x_tm, ogd_tm, ogm_tm, conv0_tm, h0) + tuple(weights)
    outs = [jax.ShapeDtypeStruct((n, width), F32), jax.ShapeDtypeStruct((bsz, width), F32),
            jax.ShapeDtypeStruct((tail, bsz, width), F32)]
    return pl.pallas_call(
        functools.partial(_lru_sample_kernel, n_steps=n_steps),
        grid=(1,),
        in_specs=[full(a) for a in ins],
        out_specs=[full(o) for o in outs],
        out_shape=outs,
        compiler_params=pltpu.CompilerParams(dimension_semantics=("arbitrary",), vmem_limit_bytes=VMEM_LIMIT),
        name="lru_sample",
    )(*ins)


def _rope_rows():
    masks = np.zeros((SUBLANES, LANES), np.float32)
    hd, hm = DIFF_ROT // 2, MLA_ROPE // 2
    for blk in range(LANES // DIFF_DIM):
        o = blk * DIFF_DIM
        masks[1, o + hd:o + 2 * hd] = 1.0
        masks[2, o:o + hd] = -1.0
    masks[4, MLA_NOPE + hm:MLA_NOPE + 2 * hm] = 1.0
    masks[5, MLA_NOPE:MLA_NOPE + hm] = -1.0
    inv_d = ROPE_THETA ** (-jnp.arange(0, DIFF_ROT, 2, dtype=F32) / DIFF_ROT)
    inv_m = ROPE_THETA ** (-jnp.arange(0, MLA_ROPE, 2, dtype=F32) / MLA_ROPE)
    freq_d = jnp.tile(jnp.concatenate([inv_d, inv_d, jnp.zeros((DIFF_DIM - DIFF_ROT,), F32)]), LANES // DIFF_DIM)
    freq_m = jnp.concatenate([jnp.zeros((MLA_NOPE,), F32), inv_m, inv_m,
                              jnp.zeros((LANES - MLA_NOPE - MLA_ROPE,), F32)])
    return jnp.asarray(masks).at[0].set(freq_d).at[3].set(freq_m)


def _seg_matrix(seg):
    idx = np.arange(MXU_DIM) // seg
    return jnp.asarray((idx[:, None] == idx[None, :]).astype(np.float32), dtype=BF16)


def _even_params(norm, w_in, diff_q_norm, diff_k_norm, q_a_norm, w_uq, mla_q_norm, kv_norm, k_rope_norm, w_uk, w_uv):
    d_model = w_in.shape[0]
    q_lora, kv_lora = w_uq.shape[0], w_uk.shape[0]
    head = MLA_NOPE + MLA_ROPE
    pad = LANES - head
    c_kr = 3 * 512 + q_lora + kv_lora
    w = jnp.concatenate([w_in[:, :c_kr], w_in[:, c_kr + MLA_ROPE:], jnp.zeros((d_model, MLA_NOPE), F32),
                         w_in[:, c_kr:c_kr + MLA_ROPE], jnp.zeros((d_model, pad), F32)], axis=1).astype(BF16)

    def gains(q_scale):
        g = jnp.zeros((SUBLANES, 2 * LANES), F32)
        g = g.at[0, :LANES].set(jnp.tile(diff_q_norm, LANES // DIFF_DIM) * (DIFF_SCALE * q_scale))
        g = g.at[1, :LANES].set(jnp.tile(diff_k_norm, LANES // DIFF_DIM))
        g = g.at[2, :head].set(mla_q_norm * (MLA_SCALE * q_scale))
        g = g.at[3, MLA_NOPE:head].set(k_rope_norm)
        g = g.at[4, :q_lora].set(q_a_norm)
        return g.at[5, :kv_lora].set(kv_norm)

    wuq = jnp.pad(w_uq.reshape(q_lora, MLA_HEADS, head), ((0, 0), (0, 0), (0, pad))).reshape(q_lora, -1).astype(BF16)
    w_knope = jnp.pad(w_uk, ((0, 0), (0, 0), (0, LANES - MLA_NOPE))).reshape(kv_lora, -1)
    w_kv = jnp.concatenate([w_knope, w_uv.reshape(kv_lora, -1)], axis=1).astype(BF16)
    w_lat = jnp.pad(jnp.transpose(w_uk, (1, 2, 0)), ((0, 0), (0, LANES - MLA_NOPE), (0, 0))).astype(BF16)
    head_p = (norm.reshape(1, -1), w, _rope_rows())
    tail_p = (_seg_matrix(DIFF_DIM), _seg_matrix(LANES), wuq)
    return head_p + (gains(LOG2E),) + tail_p + (w_kv,), head_p + (gains(1.0),) + tail_p + (w_lat,)


def _lam_rows(lq1, lk1, lq2, lk2):
    rows = jnp.zeros((SUBLANES, LANES), F32)
    for r, v in enumerate((lq1, lk1, lq2, lk2)):
        rows = rows.at[r, :v.shape[0]].set(v)
    return rows


def _even_layer(xp, xs, pools, page_table, li, lam_init, p):
    bp, sp, d = xp.shape
    bs, ts, _ = xs.shape
    page_len = pools[2].shape[2]
    past_len = page_table.shape[1] * page_len
    par_p, par_s = _even_params(p['norm'], p['w_in'], p['diff_q_norm'], p['diff_k_norm'], p['q_a_norm'], p['w_uq'],
                                p['mla_q_norm'], p['kv_norm'], p['k_rope_norm'], p['w_uk'], p['w_uv'])
    lamp = _lam_rows(p['lq1'], p['lk1'], p['lq2'], p['lk2'])
    subln = p['subln'].reshape(1, -1)

    (dq, dk, dv, dkb, dvt, qm, ckv, kr, km, vmt, gate) = _even_in_call(
        xp.reshape(bp * sp, d), par_p, prompt=True, seq=sp, pos_base=0, pos_period=sp, pos_div=False)
    ogd_p = _flash_call(dq, dkb, dvt, gate, lamp, subln, batch=bp, seq=sp, n_maps=2, gate_col0=0,
                        lam_init=lam_init, name="flash_diff")
    ogm_p = _flash_call(qm, km, vmt, gate, lamp, subln, batch=bp, seq=sp, n_maps=1, gate_col0=DIFF_HEADS,
                        lam_init=lam_init, name="flash_mla")
    rows_p = (dk.reshape(bp, sp, DIFF_HEADS, 2, DIFF_DIM), dv.reshape(bp, sp, DIFF_HEADS, DIFF_VDIM),
              ckv.reshape(bp, sp, -1), kr.reshape(bp, sp, -1))

    (sdq, sdk, sdv, sqlat, sqrope, sckv, skr, sgate) = _even_in_call(
        xs.reshape(bs * ts, d), par_s, prompt=False, seq=bs * ts, pos_base=past_len, pos_period=ts, pos_div=False)
    tp = _round_up(ts, SUBLANES)
    pad_axis = lambda a, ax, n: jnp.pad(a, [(0, n - a.shape[ax]) if i == ax else (0, 0) for i in range(a.ndim)])
    n_blocks = 2 * DIFF_HEADS
    qbd = jnp.einsum('btgd,gk->bgtkd', sdq.reshape(bs, ts, n_blocks, DIFF_DIM), jnp.eye(n_blocks, dtype=F32))
    qbd = pad_axis(qbd, 2, tp).reshape(bs, n_blocks * tp, n_blocks * DIFF_DIM)
    kv_lora = sckv.shape[1]
    ql = jnp.transpose(pad_axis(sqlat.reshape(bs, ts, MLA_HEADS, kv_lora), 1, tp), (0, 2, 1, 3))
    qr = jnp.transpose(pad_axis(sqrope.reshape(bs, ts, MLA_HEADS, MLA_ROPE), 1, tp), (0, 2, 1, 3))
    knt = pad_axis(jnp.transpose(sdk.reshape(bs, ts, -1), (0, 2, 1)), 2, page_len)
    vn = pad_axis(sdv.reshape(bs, ts * DIFF_HEADS, DIFF_VDIM), 1, page_len * DIFF_HEADS)
    cn = pad_axis(sckv.reshape(bs, ts, -1), 1, page_len)
    rnt = pad_axis(jnp.transpose(skr.reshape(bs, ts, -1), (0, 2, 1)), 2, page_len)
    ogd_s, ogm_s = _paged_call(page_table, qbd, ql.reshape(bs, MLA_HEADS * tp, kv_lora),
                               qr.reshape(bs, MLA_HEADS * tp, MLA_ROPE), knt, vn, cn, rnt,
                               pad_axis(sgate.reshape(bs, ts, -1), 1, tp), lamp, subln,
                               jnp.transpose(p['w_uv'], (1, 0, 2)), pools, li=li, n_new=ts, lam_init=lam_init)
    rows_s = (sdk.reshape(bs, ts, DIFF_HEADS, 2, DIFF_DIM), sdv.reshape(bs, ts, DIFF_HEADS, DIFF_VDIM),
              sckv.reshape(bs, ts, -1), skr.reshape(bs, ts, -1))
    return (ogd_p, ogm_p), (ogd_s, ogm_s), rows_p, rows_s


def _odd_layer(xp, xs, og_p, og_s, w_out_even, state_h, state_conv, p):
    bp, sp, d = xp.shape
    bs, ts, _ = xs.shape
    n_diff = DIFF_HEADS * DIFF_VDIM
    width = p['w_out'].shape[0]
    weights = (w_out_even[:n_diff].astype(BF16), w_out_even[n_diff:].astype(BF16), p['norm'].reshape(1, -1),
               p['w_in'].astype(BF16), p['conv_w'], p['conv_b'].reshape(1, -1), p['w_gx'].astype(BF16),
               p['b_gx'].reshape(1, -1), p['w_ga'].astype(BF16), p['b_ga'].reshape(1, -1), p['lam'].reshape(1, -1),
               p['w_out'].astype(BF16))
    tail = CONV_WIDTH - 1
    yp, hp, cp = _lru_prompt_call(xp.reshape(bp * sp, d), og_p[0], og_p[1], jnp.zeros((bp, tail, width), F32),
                                  jnp.zeros((bp, width), F32), weights, batch=bp, seq=sp)
    tmaj = lambda a: jnp.transpose(a, (1, 0, 2)).reshape(ts * bs, a.shape[2])
    ys, hs, cs = _lru_sample_call(tmaj(xs), tmaj(og_s[0]), tmaj(og_s[1]), jnp.transpose(state_conv, (1, 0, 2)),
                                  state_h, weights, n_steps=ts)
    ys = jnp.transpose(ys.reshape(ts, bs, d), (1, 0, 2))
    return yp.reshape(bp, sp, d), ys, (hp.reshape(bp, width), cp), (hs, jnp.transpose(cs, (1, 0, 2)))


def kernel(x_prompt, x_sample, cache_diff_k, cache_diff_v, cache_mla_ckv, cache_mla_krope, state_lru_h, state_lru_conv, page_table, norm_even, w_in_even, diff_q_norm, diff_k_norm, diff_lambda_q1, diff_lambda_k1, diff_lambda_q2, diff_lambda_k2, diff_subln, mla_q_a_norm, mla_w_uq, mla_q_norm, mla_kv_norm, mla_k_rope_norm, mla_w_uk, mla_w_uv, w_out_even, norm_odd, w_in_odd, lru_conv_w, lru_conv_b, lru_w_gx, lru_b_gx, lru_w_ga, lru_b_ga, lru_lambda, w_out_odd):
    n_even, n_odd = norm_even.shape[0], norm_odd.shape[0]
    assert n_even == n_odd, "layers come in (even, odd) pairs"
    n_pool, page_len = cache_diff_k.shape[1], cache_diff_k.shape[2]
    pools = (jnp.transpose(cache_diff_k, (0, 1, 3, 4, 5, 2)).reshape(n_even, n_pool, -1, page_len),
             cache_diff_v.reshape(n_even, n_pool, page_len * DIFF_HEADS, DIFF_VDIM),
             cache_mla_ckv, jnp.transpose(cache_mla_krope, (0, 1, 3, 2)))
    yp, ys = x_prompt, x_sample
    rows_p, rows_s, lru_p, lru_s = [], [], [], []
    for li in range(n_even):
        layer = 2 * li
        lam_init = 0.8 - 0.6 * math.exp(-0.3 * layer)
        pe = {'norm': norm_even[li], 'w_in': w_in_even[li], 'diff_q_norm': diff_q_norm[li],
              'diff_k_norm': diff_k_norm[li], 'lq1': diff_lambda_q1[li], 'lk1': diff_lambda_k1[li],
              'lq2': diff_lambda_q2[li], 'lk2': diff_lambda_k2[li], 'subln': diff_subln[li],
              'q_a_norm': mla_q_a_norm[li], 'w_uq': mla_w_uq[li], 'mla_q_norm': mla_q_norm[li],
              'kv_norm': mla_kv_norm[li], 'k_rope_norm': mla_k_rope_norm[li], 'w_uk': mla_w_uk[li],
              'w_uv': mla_w_uv[li]}
        og_p, og_s, r_p, r_s = _even_layer(yp, ys, pools, page_table, li, lam_init, pe)
        rows_p.append(r_p)
        rows_s.append(r_s)
        po = {'norm': norm_odd[li], 'w_in': w_in_odd[li], 'conv_w': lru_conv_w[li], 'conv_b': lru_conv_b[li],
              'w_gx': lru_w_gx[li], 'b_gx': lru_b_gx[li], 'w_ga': lru_w_ga[li], 'b_ga': lru_b_ga[li],
              'lam': lru_lambda[li], 'w_out': w_out_odd[li]}
        yp, ys, l_p, l_s = _odd_layer(yp, ys, og_p, og_s, w_out_even[li], state_lru_h[li], state_lru_conv[li], po)
        lru_p.append(l_p)
        lru_s.append(l_s)
    stack = lambda rows, k: jnp.stack([r[k] for r in rows])
    return (yp, ys, stack(rows_p, 0), stack(rows_p, 1), stack(rows_p, 2), stack(rows_p, 3), stack(lru_p, 0),
            stack(lru_p, 1), stack(rows_s, 0), stack(rows_s, 1), stack(rows_s, 2), stack(rows_s, 3), stack(lru_s, 0),
            stack(lru_s, 1))
```

```python
import functools
import math

import numpy as np
import jax
import jax.numpy as jnp
from jax import lax
from jax.experimental import pallas as pl
from jax.experimental.pallas import tpu as pltpu

DIFF_HEADS = 4
DIFF_DIM = 64
DIFF_VDIM = 2 * DIFF_DIM
DIFF_ROT = DIFF_DIM // 4
DIFF_SCALE = DIFF_DIM ** -0.5
MLA_HEADS = 4
MLA_NOPE = 64
MLA_ROPE = 32
MLA_VDIM = 128
MLA_SCALE = (MLA_NOPE + MLA_ROPE) ** -0.5
CONV_WIDTH = 4
LRU_C = 8.0
ROPE_THETA = 500000.0
EPS = 1e-6
LOG2E = math.log2(math.e)

LANES = 128
SUBLANES = 8
MXU_DIM = 256
VMEM_LIMIT = 52 * 1024 * 1024

TM_IN = 256
TQ_ATTN = 512
Q_STRIP = 256
CHUNKS_PER_TRIP = 4
TT_LRU = 256
PAGES_PER_STEP = 8

F32 = jnp.float32
BF16 = jnp.bfloat16


def _round_up(n, m):
    return (n + m - 1) // m * m


def _dot(a, b):
    return jnp.dot(a, b, preferred_element_type=F32)


def _dot_nt(a, b):
    return lax.dot_general(a, b, (((1,), (1,)), ((), ())), preferred_element_type=F32)


def _rms(x, inv_n=None):
    ms = jnp.mean(x * x, axis=-1, keepdims=True) if inv_n is None else jnp.sum(x * x, axis=-1, keepdims=True) * inv_n
    return x * lax.rsqrt(ms + EPS)


def _silu(x):
    return x * jax.nn.sigmoid(x)


def _even_in_kernel(x_ref, norm_ref, w_ref, rope_ref, gains_ref, seg64_ref, seg128_ref, wuq_ref, wx_ref,
                    *out_refs, prompt, tm, pos_base, pos_period, pos_div):
    x = x_ref[...]
    h = (_rms(x) * norm_ref[...]).astype(BF16)
    u = _dot(h, w_ref[...])

    row = pl.program_id(0) * tm + lax.broadcasted_iota(jnp.int32, (tm, LANES), 0)
    pos = (row // pos_period if pos_div else row % pos_period) + pos_base
    posf = pos.astype(F32)

    def tables(k):
        ang = posf * rope_ref[3 * k:3 * k + 1, :]
        s = jnp.sin(ang)
        return jnp.cos(ang), s * rope_ref[3 * k + 1:3 * k + 2, :], s * rope_ref[3 * k + 2:3 * k + 3, :]

    def rope(v, tabs, half):
        c, sa, sb = tabs
        return v * c + pltpu.roll(v, half, 1) * sa + pltpu.roll(v, LANES - half, 1) * sb

    def seg_sum(v, seg_ref):
        sq = v * v
        hi = sq.astype(BF16)
        lo = (sq - hi.astype(F32)).astype(BF16)
        return _dot(hi, seg_ref[...]) + _dot(lo, seg_ref[...])

    def head_norm_rope(v, seg_ref, inv_n, gain, tabs, half):
        slabs = []
        for c in range(v.shape[1] // MXU_DIM):
            vc = v[:, MXU_DIM * c:MXU_DIM * (c + 1)]
            y = vc * lax.rsqrt(seg_sum(vc, seg_ref) * inv_n + EPS)
            for s in range(MXU_DIM // LANES):
                slabs.append(rope(y[:, LANES * s:LANES * (s + 1)] * gain, tabs, half))
        return slabs

    tab_d = tables(0)
    tab_m = tables(1)
    dq = head_norm_rope(u[:, 0:512], seg64_ref, 1.0 / DIFF_DIM, gains_ref[0:1, 0:LANES], tab_d, DIFF_ROT // 2)
    dk = head_norm_rope(u[:, 512:1024], seg64_ref, 1.0 / DIFF_DIM, gains_ref[1:2, 0:LANES], tab_d, DIFF_ROT // 2)
    dv = u[:, 1024:1536]
    cqn = _rms(u[:, 1536:1792]) * gains_ref[4:5, :]
    q = _dot(cqn.astype(BF16), wuq_ref[...])
    qm = head_norm_rope(q, seg128_ref, 1.0 / (MLA_NOPE + MLA_ROPE), gains_ref[2:3, 0:LANES], tab_m, MLA_ROPE // 2)
    ckv = _rms(u[:, 1792:2048]) * gains_ref[5:6, :]
    gate = u[:, 2048:3072]
    kr_slab = rope(_rms(u[:, 3072:3200], 1.0 / MLA_ROPE) * gains_ref[3:4, 0:LANES], tab_m, MLA_ROPE // 2)
    kr = pltpu.roll(kr_slab, LANES - MLA_NOPE, 1)[:, 0:MLA_ROPE]

    def put(ref, slabs):
        for k, s in enumerate(slabs):
            ref[:, LANES * k:LANES * (k + 1)] = s.astype(ref.dtype)

    if prompt:
        dq_o, dkt_o, dv_o, dkb_o, dvt_o, qm_o, ckv_o, kr_o, km_o, vmt_o, gate_o = out_refs
        put(dq_o, dq)
        for k, s in enumerate(dk):
            dkt_o[LANES * k:LANES * (k + 1), :] = s.T
        put(dkb_o, dk)
        for hh in range(DIFF_HEADS):
            dv_o[pl.ds(hh, tm, stride=DIFF_HEADS), :] = dv[:, DIFF_VDIM * hh:DIFF_VDIM * (hh + 1)]
        dvt_o[...] = dv.T.astype(BF16)
        put(qm_o, qm)
        ckv_o[...] = ckv
        kr_o[...] = kr
        kv = _dot(ckv.astype(BF16), wx_ref[...])
        put(km_o, [kv[:, LANES * hh:LANES * (hh + 1)] + kr_slab for hh in range(MLA_HEADS)])
        vmt_o[...] = kv[:, 512:1024].T.astype(BF16)
        gate_o[...] = gate
    else:
        dq_o, dk_o, dv_o, qlat_o, qrope_o, ckv_o, kr_o, gate_o = out_refs
        put(dq_o, dq)
        put(dk_o, dk)
        dv_o[...] = dv
        ckv_o[...] = ckv
        kr_o[...] = kr
        gate_o[...] = gate
        lane = lax.broadcasted_iota(jnp.int32, (tm, LANES), 1)
        qr = jnp.zeros((tm, LANES), F32)
        for hh in range(MLA_HEADS):
            qlat_o[:, MXU_DIM * hh:MXU_DIM * (hh + 1)] = _dot(qm[hh].astype(BF16), wx_ref[hh])
            moved = pltpu.roll(qm[hh], (MLA_ROPE * hh + LANES - MLA_NOPE) % LANES, 1)
            qr = qr + jnp.where((lane >= MLA_ROPE * hh) & (lane < MLA_ROPE * (hh + 1)), moved, 0.0)
        qrope_o[...] = qr


def _even_in_call(x2d, params, *, prompt, seq, pos_base, pos_period, pos_div):
    n, d = x2d.shape
    tm = min(TM_IN, seq)
    assert seq % tm == 0 and n % seq == 0
    nt = seq // tm
    (norm, w, rope_c, gains, seg64, seg128, wuq, wx) = params
    full = lambda a: pl.BlockSpec(a.shape, lambda i, nd=a.ndim: (0,) * nd)
    tile = lambda c: pl.BlockSpec((tm, c), lambda i: (i, 0))
    tile_t = pl.BlockSpec((512, tm), lambda i: (i // nt, i % nt))
    shape_t = lambda t: jax.ShapeDtypeStruct((n // seq * 512, seq), t)
    tile_v = pl.BlockSpec((tm * DIFF_HEADS, DIFF_VDIM), lambda i: (i, 0))
    shape_v = jax.ShapeDtypeStruct((n * DIFF_HEADS, DIFF_VDIM), F32)
    if prompt:
        outs = [(512, BF16), F32, None, (512, BF16), BF16, (512, BF16), (256, F32),
                (MLA_ROPE, F32), (512, BF16), BF16, (1024, F32)]
    else:
        outs = [(512, F32), (512, F32), (512, F32), (1024, F32), (LANES, F32), (256, F32), (MLA_ROPE, F32),
                (1024, F32)]
    kern = functools.partial(_even_in_kernel, prompt=prompt, tm=tm, pos_base=pos_base, pos_period=pos_period,
                             pos_div=pos_div)
    return pl.pallas_call(
        kern,
        grid=(n // tm,),
        in_specs=[tile(d), full(norm), full(w), full(rope_c), full(gains), full(seg64), full(seg128), full(wuq),
                  full(wx)],
        out_specs=[tile(o[0]) if isinstance(o, tuple) else tile_v if o is None else tile_t for o in outs],
        out_shape=[jax.ShapeDtypeStruct((n, o[0]), o[1]) if isinstance(o, tuple) else shape_v if o is None
                   else shape_t(o) for o in outs],
        compiler_params=pltpu.CompilerParams(dimension_semantics=("arbitrary",), vmem_limit_bytes=VMEM_LIMIT),
        name="even_in_prompt" if prompt else "even_in_sample",
    )(x2d, norm, w, rope_c, gains, seg64, seg128, wuq, wx)


def _flash_kernel(q_ref, k_ref, vt_ref, gate_ref, lamp_ref, subln_ref, o_ref, m_sc, l_sc, acc_sc, *,
                  n_maps, tq, lam_init):
    i = pl.program_id(2)
    strip = min(Q_STRIP, tq)
    q = q_ref[...]
    if n_maps == 2:
        lane = lax.broadcasted_iota(jnp.int32, q.shape, 1)
        qf = q.astype(F32)
        qs = [jnp.where(lane < DIFF_DIM, qf, 0.0).astype(BF16), jnp.where(lane >= DIFF_DIM, qf, 0.0).astype(BF16)]
    else:
        qs = [q]
    m_sc[...] = jnp.full(m_sc.shape, -jnp.inf, F32)
    l_sc[...] = jnp.zeros(l_sc.shape, F32)
    acc_sc[...] = jnp.zeros(acc_sc.shape, F32)

    def scores(j):
        k = k_ref[pl.ds(pl.multiple_of(j * tq, tq), tq), :]
        return tuple(_dot_nt(k, qs[a]) for a in range(n_maps))

    def update(j, ss, masked):
        vt = vt_ref[:, pl.ds(pl.multiple_of(j * tq, tq), tq)]
        for a in range(n_maps):
            for c in range(tq // strip):
                cols = slice(strip * c, strip * (c + 1))
                s = ss[a][:, cols]
                if masked:
                    key_id = lax.broadcasted_iota(jnp.int32, s.shape, 0)
                    qry_id = lax.broadcasted_iota(jnp.int32, s.shape, 1) + strip * c
                    s = jnp.where(key_id <= qry_id, s, -jnp.inf)
                m_prev = m_sc[a, :, cols]
                m_new = jnp.maximum(m_prev, jnp.max(s, axis=0, keepdims=True))
                alpha = jnp.exp2(m_prev - m_new)
                p = jnp.exp2(s - m_new)
                l_sc[a, :, cols] = alpha * l_sc[a, :, cols] + jnp.sum(p, axis=0, keepdims=True)
                acc_sc[a, :, cols] = alpha * acc_sc[a, :, cols] + _dot(vt, p.astype(BF16))
                m_sc[a, :, cols] = m_new

    def group(jj, ss):
        for u in range(CHUNKS_PER_TRIP):
            j = jj * CHUNKS_PER_TRIP + u
            nxt = scores(j + 1)
            update(j, ss, False)
            ss = nxt
        return ss

    def single(j, ss):
        nxt = scores(j + 1)
        update(j, ss, False)
        return nxt

    n_groups = i // CHUNKS_PER_TRIP
    ss = lax.fori_loop(0, n_groups, group, scores(0))
    ss = lax.fori_loop(n_groups * CHUNKS_PER_TRIP, i, single, ss)
    update(i, ss, True)

    g = _silu(gate_ref[...])
    if n_maps == 2:
        lp = lamp_ref[...]
        lam = (jnp.exp(jnp.sum(lp[0:1] * lp[1:2], axis=1, keepdims=True))
               - jnp.exp(jnp.sum(lp[2:3] * lp[3:4], axis=1, keepdims=True)) + lam_init)
        d = (acc_sc[0] / l_sc[0] - lam * (acc_sc[1] / l_sc[1])).T
        o = (_rms(d) * subln_ref[...]) * (1.0 - lam_init)
    else:
        o = (acc_sc[0] / l_sc[0]).T
    o_ref[...] = (o * g).astype(o_ref.dtype)


def _flash_call(q, k, vt, gate, lamp, subln, *, batch, seq, n_maps, gate_col0, lam_init, name):
    n, width = q.shape
    heads = width // LANES
    tq = min(TQ_ATTN, seq)
    assert seq % tq == 0
    nq = seq // tq
    kern = functools.partial(_flash_kernel, n_maps=n_maps, tq=tq, lam_init=lam_init)
    qspec = pl.BlockSpec((tq, LANES), lambda b, h, i: (b * nq + i, h))
    return pl.pallas_call(
        kern,
        grid=(batch, heads, nq),
        in_specs=[qspec,
                  pl.BlockSpec((seq, LANES), lambda b, h, i: (b, h)),
                  pl.BlockSpec((LANES, seq), lambda b, h, i: (b * heads + h, 0)),
                  pl.BlockSpec((tq, LANES), lambda b, h, i: (b * nq + i, gate_col0 + h)),
                  pl.BlockSpec(lamp.shape, lambda b, h, i: (0, 0)),
                  pl.BlockSpec(subln.shape, lambda b, h, i: (0, 0))],
        out_specs=qspec,
        out_shape=jax.ShapeDtypeStruct((n, width), BF16),
        scratch_shapes=[pltpu.VMEM((n_maps, 1, tq), F32), pltpu.VMEM((n_maps, 1, tq), F32),
                        pltpu.VMEM((n_maps, LANES, tq), F32)],
        compiler_params=pltpu.CompilerParams(dimension_semantics=("arbitrary",) * 3, vmem_limit_bytes=VMEM_LIMIT),
        name=name,
    )(q, k, vt, gate, lamp, subln)


def _paged_kernel(pt_ref, qbd_ref, ql_ref, qr_ref, knt_ref, vn_ref, cn_ref, rnt_ref, gate_ref, lamp_ref, subln_ref,
                  wuv_ref, *rest, n_new, tp, pages, lam_init):
    del pt_ref
    page_refs = rest[:4 * pages]
    ogd_ref, ogm_ref, m_sc, l_sc, accd, accm = rest[4 * pages:]
    p = pl.program_id(1)
    page_len = cn_ref.shape[0]
    rows_h = 2 * tp
    n_d = DIFF_HEADS * rows_h

    def attend(kts, vs, cs, rts, mask):
        cat = lambda xs: xs[0] if len(xs) == 1 else jnp.concatenate(xs, axis=1)
        cvals = [c[...] for c in cs]
        sd = cat([_dot(qbd_ref[...], kt[...]) for kt in kts])
        sm = cat([_dot_nt(ql_ref[...], cv) + _dot(qr_ref[...], rt[...]) for cv, rt in zip(cvals, rts)])
        s = jnp.concatenate([sd, sm], axis=0)
        if mask is not None:
            s = jnp.where(mask, s, -jnp.inf)
        m_prev = m_sc[...]
        m_new = jnp.maximum(m_prev, jnp.max(s, axis=1, keepdims=True))
        alpha = jnp.exp(m_prev - m_new)
        pr = jnp.exp(s - m_new)
        l_sc[...] = alpha * l_sc[...] + jnp.sum(pr, axis=1, keepdims=True)
        m_sc[...] = m_new
        pvs = []
        for h in range(DIFF_HEADS):
            acc = None
            for g, v in enumerate(vs):
                t = _dot(pr[rows_h * h:rows_h * (h + 1), page_len * g:page_len * (g + 1)],
                         v[pl.ds(h, page_len, stride=DIFF_HEADS), :])
                acc = t if acc is None else acc + t
            pvs.append(acc)
        accd[...] = alpha[0:n_d] * accd[...] + jnp.concatenate(pvs, axis=0)
        acc = None
        for g, cv in enumerate(cvals):
            t = _dot(pr[n_d:, page_len * g:page_len * (g + 1)], cv)
            acc = t if acc is None else acc + t
        accm[...] = alpha[n_d:] * accm[...] + acc

    @pl.when(p == 0)
    def _():
        m_sc[...] = jnp.full(m_sc.shape, -jnp.inf, F32)
        l_sc[...] = jnp.zeros(l_sc.shape, F32)
        accd[...] = jnp.zeros(accd.shape, F32)
        accm[...] = jnp.zeros(accm.shape, F32)
        t_row = lax.broadcasted_iota(jnp.int32, (m_sc.shape[0], page_len), 0) % tp
        key = lax.broadcasted_iota(jnp.int32, (m_sc.shape[0], page_len), 1)
        attend([knt_ref], [vn_ref], [cn_ref], [rnt_ref], (key <= t_row) & (key < n_new))

    attend([page_refs[4 * g] for g in range(pages)], [page_refs[4 * g + 1] for g in range(pages)],
           [page_refs[4 * g + 2] for g in range(pages)], [page_refs[4 * g + 3] for g in range(pages)], None)

    @pl.when(p == pl.num_programs(1) - 1)
    def _():
        gate = _silu(gate_ref[...])
        lp = lamp_ref[...]
        lam = (jnp.exp(jnp.sum(lp[0:1] * lp[1:2], axis=1, keepdims=True))
               - jnp.exp(jnp.sum(lp[2:3] * lp[3:4], axis=1, keepdims=True)) + lam_init)
        l = l_sc[...]
        av = accd[...] / l[0:n_d]
        lat = accm[...] / l[n_d:]
        for h in range(DIFF_HEADS):
            d = av[rows_h * h:rows_h * h + tp] - lam * av[rows_h * h + tp:rows_h * (h + 1)]
            o = (_rms(d) * subln_ref[...]) * (1.0 - lam_init)
            ogd_ref[:, LANES * h:LANES * (h + 1)] = (o * gate[:, LANES * h:LANES * (h + 1)])[0:n_new]
        for h in range(MLA_HEADS):
            o = _dot(lat[tp * h:tp * (h + 1)], wuv_ref[h])
            c0 = DIFF_HEADS * DIFF_VDIM + MLA_VDIM * h
            ogm_ref[:, MLA_VDIM * h:MLA_VDIM * (h + 1)] = (o * gate[:, c0:c0 + MLA_VDIM])[0:n_new]


def _paged_call(page_table, qbd, ql, qr, knt, vn, cn, rnt, gate, lamp, subln, wuv, pools, *, li, n_new, lam_init):
    bsz, n_pages = page_table.shape
    tp = gate.shape[1]
    pages = math.gcd(PAGES_PER_STEP, n_pages)
    steps = n_pages // pages
    kern = functools.partial(_paged_kernel, n_new=n_new, tp=tp, pages=pages, lam_init=lam_init)

    def seq_spec(a):
        return pl.BlockSpec((None,) + a.shape[1:], lambda b, p, pt, nd=a.ndim: (b,) + (0,) * (nd - 1))

    def const_spec(a):
        return pl.BlockSpec(a.shape, lambda b, p, pt, nd=a.ndim: (0,) * nd)

    def page_spec(pool, g):
        return pl.BlockSpec((None, None) + pool.shape[2:],
                            lambda b, p, pt, g=g: (li, pt[b * n_pages + p * pages + g], 0, 0))

    page_specs, page_args = [], []
    for g in range(pages):
        for pool in pools:
            page_specs.append(page_spec(pool, g))
            page_args.append(pool)
    out_d = jax.ShapeDtypeStruct((bsz, n_new, DIFF_HEADS * DIFF_VDIM), F32)
    out_m = jax.ShapeDtypeStruct((bsz, n_new, MLA_HEADS * MLA_VDIM), F32)
    out_spec = lambda s: pl.BlockSpec((None,) + s.shape[1:], lambda b, p, pt: (b, 0, 0))
    n_rows = qbd.shape[1] + ql.shape[1]
    grid_spec = pltpu.PrefetchScalarGridSpec(
        num_scalar_prefetch=1,
        grid=(bsz, steps),
        in_specs=[seq_spec(a) for a in (qbd, ql, qr, knt, vn, cn, rnt, gate)]
        + [const_spec(a) for a in (lamp, subln, wuv)] + page_specs,
        out_specs=[out_spec(out_d), out_spec(out_m)],
        scratch_shapes=[pltpu.VMEM((n_rows, 1), F32), pltpu.VMEM((n_rows, 1), F32),
                        pltpu.VMEM((qbd.shape[1], DIFF_VDIM), F32), pltpu.VMEM((ql.shape[1], cn.shape[2]), F32)],
    )
    return pl.pallas_call(
        kern,
        grid_spec=grid_spec,
        out_shape=[out_d, out_m],
        compiler_params=pltpu.CompilerParams(dimension_semantics=("arbitrary", "arbitrary"),
                                             vmem_limit_bytes=VMEM_LIMIT),
        name="paged_attention",
    )(page_table.reshape(-1), qbd, ql, qr, knt, vn, cn, rnt, gate, lamp, subln, wuv, *page_args)


def _lru_front(x, ogd, ogm, wd_ref, wm_ref, norm_ref, win_ref):
    x1 = x + _dot(ogd.astype(BF16), wd_ref[...]) + _dot(ogm.astype(BF16), wm_ref[...])
    hn = (_rms(x1) * norm_ref[...]).astype(BF16)
    return x1, _dot(hn, win_ref[...])


def _lru_gates(conv, blk, wgx_ref, bgx_ref, wga_ref, bga_ref, lam_ref):
    w = wgx_ref.shape[1]
    cs = slice(w * blk, w * (blk + 1))
    c = conv[:, cs]
    cb = c.astype(BF16)
    gx = jax.nn.sigmoid(_dot(cb, wgx_ref[blk]) + bgx_ref[:, cs])
    ga = jax.nn.sigmoid(_dot(cb, wga_ref[blk]) + bga_ref[:, cs])
    log_a = (LRU_C * ga) * jax.nn.log_sigmoid(lam_ref[:, cs])
    th = jnp.tanh(log_a)
    mult = jnp.sqrt(-2.0 * th / (1.0 - th))
    return jnp.exp(log_a), mult * (gx * c)


def _lru_prompt_kernel(x_ref, ogd_ref, ogm_ref, conv0_ref, h0_ref, wd_ref, wm_ref, norm_ref, win_ref, cw_ref,
                       cbias_ref, wgx_ref, bgx_ref, wga_ref, bga_ref, lam_ref, wout_ref,
                       y_ref, hout_ref, cout_ref, ext_sc, a_sc, b_sc, hc_sc, *, tt):
    width = x_ref.shape[1]
    tail = CONV_WIDTH - 1
    base = SUBLANES - tail

    @pl.when(pl.program_id(1) == 0)
    def _():
        ext_sc[base:SUBLANES, :] = conv0_ref[...]
        hc_sc[...] = jnp.broadcast_to(h0_ref[...], hc_sc.shape)

    x1, u = _lru_front(x_ref[...], ogd_ref[...], ogm_ref[...], wd_ref, wm_ref, norm_ref, win_ref)
    y_ref[...] = x1
    ext_sc[SUBLANES:SUBLANES + tt, :] = u[:, 0:width]
    acc = ext_sc[base:base + tt, :] * cw_ref[0:1, :]
    for j in range(1, CONV_WIDTH):
        acc = acc + ext_sc[base + j:base + j + tt, :] * cw_ref[j:j + 1, :]
    conv = cbias_ref[...] + acc
    new_tail = ext_sc[base + tt:SUBLANES + tt, :]
    ext_sc[base:SUBLANES, :] = new_tail
    cout_ref[...] = new_tail

    n_blk = wgx_ref.shape[0]
    bw = wgx_ref.shape[1]
    sub = lax.broadcasted_iota(jnp.int32, (tt // SUBLANES, SUBLANES, bw), 1)
    for blk in range(n_blk):
        a, b = _lru_gates(conv, blk, wgx_ref, bgx_ref, wga_ref, bga_ref, lam_ref)
        a = a.reshape(tt // SUBLANES, SUBLANES, bw)
        b = b.reshape(tt // SUBLANES, SUBLANES, bw)
        d = 1
        while d < SUBLANES:
            keep = sub >= d
            a_sh = jnp.where(keep, pltpu.roll(a, d, 1), 1.0)
            b_sh = jnp.where(keep, pltpu.roll(b, d, 1), 0.0)
            b = a * b_sh + b
            a = a * a_sh
            d *= 2
        a_sc[:, bw * blk:bw * (blk + 1)] = a.reshape(tt, bw)
        b_sc[:, bw * blk:bw * (blk + 1)] = b.reshape(tt, bw)

    def group(g, hc):
        rows = pl.ds(pl.multiple_of(g * SUBLANES, SUBLANES), SUBLANES)
        hs = a_sc[rows, :] * hc + b_sc[rows, :]
        b_sc[rows, :] = hs
        return jnp.broadcast_to(hs[SUBLANES - 1:SUBLANES, :], hs.shape)

    hc = lax.fori_loop(0, tt // SUBLANES, group, hc_sc[...], unroll=4)
    hc_sc[...] = hc
    hout_ref[...] = hc[0:1, :]
    hs = b_sc[...]
    y_ref[...] += _dot((hs * _silu(u[:, width:2 * width])).astype(BF16), wout_ref[...])


def _lru_sample_kernel(x_ref, ogd_ref, ogm_ref, conv0_ref, h0_ref, wd_ref, wm_ref, norm_ref, win_ref, cw_ref,
                       cbias_ref, wgx_ref, bgx_ref, wga_ref, bga_ref, lam_ref, wout_ref,
                       y_ref, hout_ref, cout_ref, *, n_steps):
    width = x_ref.shape[1]
    bsz = x_ref.shape[0] // n_steps
    tail = CONV_WIDTH - 1
    x1, u = _lru_front(x_ref[...], ogd_ref[...], ogm_ref[...], wd_ref, wm_ref, norm_ref, win_ref)
    ext = [conv0_ref[j] for j in range(tail)] + [u[bsz * t:bsz * (t + 1), 0:width] for t in range(n_steps)]
    convs = []
    for t in range(n_steps):
        acc = ext[t] * cw_ref[0:1, :]
        for j in range(1, CONV_WIDTH):
            acc = acc + ext[t + j] * cw_ref[j:j + 1, :]
        convs.append(cbias_ref[...] + acc)
    conv = jnp.concatenate(convs, axis=0)
    for j in range(tail):
        cout_ref[j] = ext[n_steps + j]
    ab = [_lru_gates(conv, blk, wgx_ref, bgx_ref, wga_ref, bga_ref, lam_ref) for blk in range(wgx_ref.shape[0])]
    a = jnp.concatenate([p[0] for p in ab], axis=1)
    b = jnp.concatenate([p[1] for p in ab], axis=1)
    h = h0_ref[...]
    hs = []
    for t in range(n_steps):
        h = a[bsz * t:bsz * (t + 1)] * h + b[bsz * t:bsz * (t + 1)]
        hs.append(h)
    hout_ref[...] = h
    hs = jnp.concatenate(hs, axis=0)
    y_ref[...] = x1 + _dot((hs * _silu(u[:, width:2 * width])).astype(BF16), wout_ref[...])


def _lru_weight_specs(weights, index):
    return [pl.BlockSpec(w.shape, functools.partial(index, nd=w.ndim)) for w in weights]


def _lru_prompt_call(x2d, ogd, ogm, conv0, h0, weights, *, batch, seq):
    n, width = x2d.shape
    tt = min(TT_LRU, seq)
    assert seq % tt == 0
    nt = seq // tt
    tail = CONV_WIDTH - 1
    rows = lambda c: pl.BlockSpec((tt, c), lambda b, t: (b * nt + t, 0))
    state = lambda r: pl.BlockSpec((None, r, width), lambda b, t: (b, 0, 0))
    return pl.pallas_call(
        functools.partial(_lru_prompt_kernel, tt=tt),
        grid=(batch, nt),
        in_specs=[rows(width), rows(ogd.shape[1]), rows(ogm.shape[1]), state(tail), state(1)]
        + _lru_weight_specs(weights, lambda b, t, nd: (0,) * nd),
        out_specs=[rows(width), state(1), state(tail)],
        out_shape=[jax.ShapeDtypeStruct((n, width), F32), jax.ShapeDtypeStruct((batch, 1, width), F32),
                   jax.ShapeDtypeStruct((batch, tail, width), F32)],
        scratch_shapes=[pltpu.VMEM((SUBLANES + tt, width), F32), pltpu.VMEM((tt, width), F32),
                        pltpu.VMEM((tt, width), F32), pltpu.VMEM((SUBLANES, width), F32)],
        compiler_params=pltpu.CompilerParams(dimension_semantics=("arbitrary", "arbitrary"),
                                             vmem_limit_bytes=VMEM_LIMIT),
        name="lru_prompt",
    )(x2d, ogd, ogm, conv0, h0.reshape(batch, 1, width), *weights)


def _lru_sample_call(x_tm, ogd_tm, ogm_tm, conv0_tm, h0, weights, *, n_steps):
    n, width = x_tm.shape
    bsz = n // n_steps
    tail = CONV_WIDTH - 1
    full = lambda a: pl.BlockSpec(a.shape, lambda i, nd=a.ndim: (0,) * nd)
    ins = (x_tm, ogd_tm, ogm_tm, conv0_tm, h0) + tuple(weights)
    outs = [jax.ShapeDtypeStruct((n, width), F32), jax.ShapeDtypeStruct((bsz, width), F32),
            jax.ShapeDtypeStruct((tail, bsz, width), F32)]
    return pl.pallas_call(
        functools.partial(_lru_sample_kernel, n_steps=n_steps),
        grid=(1,),
        in_specs=[full(a) for a in ins],
        out_specs=[full(o) for o in outs],
        out_shape=outs,
        compiler_params=pltpu.CompilerParams(dimension_semantics=("arbitrary",), vmem_limit_bytes=VMEM_LIMIT),
        name="lru_sample",
    )(*ins)


def _rope_rows():
    masks = np.zeros((SUBLANES, LANES), np.float32)
    hd, hm = DIFF_ROT // 2, MLA_ROPE // 2
    for blk in range(LANES // DIFF_DIM):
        o = blk * DIFF_DIM
        masks[1, o + hd:o + 2 * hd] = 1.0
        masks[2, o:o + hd] = -1.0
    masks[4, MLA_NOPE + hm:MLA_NOPE + 2 * hm] = 1.0
    masks[5, MLA_NOPE:MLA_NOPE + hm] = -1.0
    inv_d = ROPE_THETA ** (-jnp.arange(0, DIFF_ROT, 2, dtype=F32) / DIFF_ROT)
    inv_m = ROPE_THETA ** (-jnp.arange(0, MLA_ROPE, 2, dtype=F32) / MLA_ROPE)
    freq_d = jnp.tile(jnp.concatenate([inv_d, inv_d, jnp.zeros((DIFF_DIM - DIFF_ROT,), F32)]), LANES // DIFF_DIM)
    freq_m = jnp.concatenate([jnp.zeros((MLA_NOPE,), F32), inv_m, inv_m,
                              jnp.zeros((LANES - MLA_NOPE - MLA_ROPE,), F32)])
    return jnp.asarray(masks).at[0].set(freq_d).at[3].set(freq_m)


def _seg_matrix(seg):
    idx = np.arange(MXU_DIM) // seg
    return jnp.asarray((idx[:, None] == idx[None, :]).astype(np.float32), dtype=BF16)


def _even_params(norm, w_in, diff_q_norm, diff_k_norm, q_a_norm, w_uq, mla_q_norm, kv_norm, k_rope_norm, w_uk, w_uv):
    d_model = w_in.shape[0]
    q_lora, kv_lora = w_uq.shape[0], w_uk.shape[0]
    head = MLA_NOPE + MLA_ROPE
    pad = LANES - head
    c_kr = 3 * 512 + q_lora + kv_lora
    w = jnp.concatenate([w_in[:, :c_kr], w_in[:, c_kr + MLA_ROPE:], jnp.zeros((d_model, MLA_NOPE), F32),
                         w_in[:, c_kr:c_kr + MLA_ROPE], jnp.zeros((d_model, pad), F32)], axis=1).astype(BF16)

    def gains(q_scale):
        g = jnp.zeros((SUBLANES, 2 * LANES), F32)
        g = g.at[0, :LANES].set(jnp.tile(diff_q_norm, LANES // DIFF_DIM) * (DIFF_SCALE * q_scale))
        g = g.at[1, :LANES].set(jnp.tile(diff_k_norm, LANES // DIFF_DIM))
        g = g.at[2, :head].set(mla_q_norm * (MLA_SCALE * q_scale))
        g = g.at[3, MLA_NOPE:head].set(k_rope_norm)
        g = g.at[4, :q_lora].set(q_a_norm)
        return g.at[5, :kv_lora].set(kv_norm)

    wuq = jnp.pad(w_uq.reshape(q_lora, MLA_HEADS, head), ((0, 0), (0, 0), (0, pad))).reshape(q_lora, -1).astype(BF16)
    w_knope = jnp.pad(w_uk, ((0, 0), (0, 0), (0, LANES - MLA_NOPE))).reshape(kv_lora, -1)
    w_kv = jnp.concatenate([w_knope, w_uv.reshape(kv_lora, -1)], axis=1).astype(BF16)
    w_lat = jnp.pad(jnp.transpose(w_uk, (1, 2, 0)), ((0, 0), (0, LANES - MLA_NOPE), (0, 0))).astype(BF16)
    head_p = (norm.reshape(1, -1), w, _rope_rows())
    tail_p = (_seg_matrix(DIFF_DIM), _seg_matrix(LANES), wuq)
    return head_p + (gains(LOG2E),) + tail_p + (w_kv,), head_p + (gains(1.0),) + tail_p + (w_lat,)


def _lam_rows(lq1, lk1, lq2, lk2):
    rows = jnp.zeros((SUBLANES, LANES), F32)
    for r, v in enumerate((lq1, lk1, lq2, lk2)):
        rows = rows.at[r, :v.shape[0]].set(v)
    return rows


def _even_layer(xp, xs, pools, page_table, li, lam_init, p):
    bp, sp, d = xp.shape
    bs, ts, _ = xs.shape
    page_len = pools[2].shape[2]
    past_len = page_table.shape[1] * page_len
    par_p, par_s = _even_params(p['norm'], p['w_in'], p['diff_q_norm'], p['diff_k_norm'], p['q_a_norm'], p['w_uq'],
                                p['mla_q_norm'], p['kv_norm'], p['k_rope_norm'], p['w_uk'], p['w_uv'])
    lamp = _lam_rows(p['lq1'], p['lk1'], p['lq2'], p['lk2'])
    subln = p['subln'].reshape(1, -1)

    (dq, dkt, dv, dkb, dvt, qm, ckv, kr, km, vmt, gate) = _even_in_call(
        xp.reshape(bp * sp, d), par_p, prompt=True, seq=sp, pos_base=0, pos_period=sp, pos_div=False)
    ogd_p = _flash_call(dq, dkb, dvt, gate, lamp, subln, batch=bp, seq=sp, n_maps=2, gate_col0=0,
                        lam_init=lam_init, name="flash_diff")
    ogm_p = _flash_call(qm, km, vmt, gate, lamp, subln, batch=bp, seq=sp, n_maps=1, gate_col0=DIFF_HEADS,
                        lam_init=lam_init, name="flash_mla")
    dk_rows = jnp.transpose(dkt.reshape(bp, DIFF_HEADS, 2, DIFF_DIM, sp), (0, 4, 1, 2, 3))
    rows_p = (dk_rows, dv.reshape(bp, sp, DIFF_HEADS, DIFF_VDIM), ckv.reshape(bp, sp, -1), kr.reshape(bp, sp, -1))

    (sdq, sdk, sdv, sqlat, sqrope, sckv, skr, sgate) = _even_in_call(
        xs.reshape(bs * ts, d), par_s, prompt=False, seq=bs * ts, pos_base=past_len, pos_period=ts, pos_div=False)
    tp = _round_up(ts, SUBLANES)
    pad_axis = lambda a, ax, n: jnp.pad(a, [(0, n - a.shape[ax]) if i == ax else (0, 0) for i in range(a.ndim)])
    n_blocks = 2 * DIFF_HEADS
    qbd = jnp.einsum('btgd,gk->bgtkd', sdq.reshape(bs, ts, n_blocks, DIFF_DIM), jnp.eye(n_blocks, dtype=F32))
    qbd = pad_axis(qbd, 2, tp).reshape(bs, n_blocks * tp, n_blocks * DIFF_DIM)
    kv_lora = sckv.shape[1]
    ql = jnp.transpose(pad_axis(sqlat.reshape(bs, ts, MLA_HEADS, kv_lora), 1, tp), (0, 2, 1, 3))
    qr = jnp.transpose(pad_axis(sqrope.reshape(bs, ts, MLA_HEADS, MLA_ROPE), 1, tp), (0, 2, 1, 3))
    knt = pad_axis(jnp.transpose(sdk.reshape(bs, ts, -1), (0, 2, 1)), 2, page_len)
    vn = pad_axis(sdv.reshape(bs, ts * DIFF_HEADS, DIFF_VDIM), 1, page_len * DIFF_HEADS)
    cn = pad_axis(sckv.reshape(bs, ts, -1), 1, page_len)
    rnt = pad_axis(jnp.transpose(skr.reshape(bs, ts, -1), (0, 2, 1)), 2, page_len)
    ogd_s, ogm_s = _paged_call(page_table, qbd, ql.reshape(bs, MLA_HEADS * tp, kv_lora),
                               qr.reshape(bs, MLA_HEADS * tp, MLA_ROPE), knt, vn, cn, rnt,
                               pad_axis(sgate.reshape(bs, ts, -1), 1, tp), lamp, subln,
                               jnp.transpose(p['w_uv'], (1, 0, 2)), pools, li=li, n_new=ts, lam_init=lam_init)
    rows_s = (sdk.reshape(bs, ts, DIFF_HEADS, 2, DIFF_DIM), sdv.reshape(bs, ts, DIFF_HEADS, DIFF_VDIM),
              sckv.reshape(bs, ts, -1), skr.reshape(bs, ts, -1))
    return (ogd_p, ogm_p), (ogd_s, ogm_s), rows_p, rows_s


def _odd_layer(xp, xs, og_p, og_s, w_out_even, state_h, state_conv, p):
    bp, sp, d = xp.shape
    bs, ts, _ = xs.shape
    n_diff = DIFF_HEADS * DIFF_VDIM
    width = p['w_out'].shape[0]
    weights = (w_out_even[:n_diff].astype(BF16), w_out_even[n_diff:].astype(BF16), p['norm'].reshape(1, -1),
               p['w_in'].astype(BF16), p['conv_w'], p['conv_b'].reshape(1, -1), p['w_gx'].astype(BF16),
               p['b_gx'].reshape(1, -1), p['w_ga'].astype(BF16), p['b_ga'].reshape(1, -1), p['lam'].reshape(1, -1),
               p['w_out'].astype(BF16))
    tail = CONV_WIDTH - 1
    yp, hp, cp = _lru_prompt_call(xp.reshape(bp * sp, d), og_p[0], og_p[1], jnp.zeros((bp, tail, width), F32),
                                  jnp.zeros((bp, width), F32), weights, batch=bp, seq=sp)
    tmaj = lambda a: jnp.transpose(a, (1, 0, 2)).reshape(ts * bs, a.shape[2])
    ys, hs, cs = _lru_sample_call(tmaj(xs), tmaj(og_s[0]), tmaj(og_s[1]), jnp.transpose(state_conv, (1, 0, 2)),
                                  state_h, weights, n_steps=ts)
    ys = jnp.transpose(ys.reshape(ts, bs, d), (1, 0, 2))
    return yp.reshape(bp, sp, d), ys, (hp.reshape(bp, width), cp), (hs, jnp.transpose(cs, (1, 0, 2)))


def kernel(x_prompt, x_sample, cache_diff_k, cache_diff_v, cache_mla_ckv, cache_mla_krope, state_lru_h, state_lru_conv, page_table, norm_even, w_in_even, diff_q_norm, diff_k_norm, diff_lambda_q1, diff_lambda_k1, diff_lambda_q2, diff_lambda_k2, diff_subln, mla_q_a_norm, mla_w_uq, mla_q_norm, mla_kv_norm, mla_k_rope_norm, mla_w_uk, mla_w_uv, w_out_even, norm_odd, w_in_odd, lru_conv_w, lru_conv_b, lru_w_gx, lru_b_gx, lru_w_ga, lru_b_ga, lru_lambda, w_out_odd):
    n_even, n_odd = norm_even.shape[0], norm_odd.shape[0]
    assert n_even == n_odd, "layers come in (even, odd) pairs"
    n_pool, page_len = cache_diff_k.shape[1], cache_diff_k.shape[2]
    pools = (jnp.transpose(cache_diff_k, (0, 1, 3, 4, 5, 2)).reshape(n_even, n_pool, -1, page_len),
             cache_diff_v.reshape(n_even, n_pool, page_len * DIFF_HEADS, DIFF_VDIM),
             cache_mla_ckv, jnp.transpose(cache_mla_krope, (0, 1, 3, 2)))
    yp, ys = x_prompt, x_sample
    rows_p, rows_s, lru_p, lru_s = [], [], [], []
    for li in range(n_even):
        layer = 2 * li
        lam_init = 0.8 - 0.6 * math.exp(-0.3 * layer)
        pe = {'norm': norm_even[li], 'w_in': w_in_even[li], 'diff_q_norm': diff_q_norm[li],
              'diff_k_norm': diff_k_norm[li], 'lq1': diff_lambda_q1[li], 'lk1': diff_lambda_k1[li],
              'lq2': diff_lambda_q2[li], 'lk2': diff_lambda_k2[li], 'subln': diff_subln[li],
              'q_a_norm': mla_q_a_norm[li], 'w_uq': mla_w_uq[li], 'mla_q_norm': mla_q_norm[li],
              'kv_norm': mla_kv_norm[li], 'k_rope_norm': mla_k_rope_norm[li], 'w_uk': mla_w_uk[li],
              'w_uv': mla_w_uv[li]}
        og_p, og_s, r_p, r_s = _even_layer(yp, ys, pools, page_table, li, lam_init, pe)
        rows_p.append(r_p)
        rows_s.append(r_s)
        po = {'norm': norm_odd[li], 'w_in': w_in_odd[li], 'conv_w': lru_conv_w[li], 'conv_b': lru_conv_b[li],
              'w_gx': lru_w_gx[li], 'b_gx': lru_b_gx[li], 'w_ga': lru_w_ga[li], 'b_ga': lru_b_ga[li],
              'lam': lru_lambda[li], 'w_out': w_out_odd[li]}
        yp, ys, l_p, l_s = _odd_layer(yp, ys, og_p, og_s, w_out_even[li], state_lru_h[li], state_lru_conv[li], po)
        lru_p.append(l_p)
        lru_s.append(l_s)
    stack = lambda rows, k: jnp.stack([r[k] for r in rows])
    return (yp, ys, stack(rows_p, 0), stack(rows_p, 1), stack(rows_p, 2), stack(rows_p, 3), stack(lru_p, 0),
            stack(lru_p, 1), stack(rows_s, 0), stack(rows_s, 1), stack(rows_s, 2), stack(rows_s, 3), stack(lru_s, 0),
            stack(lru_s, 1))
```
